```python
import jax, jax.numpy as jnp
from jax import lax
import numpy as np

D_MODEL = 2048
BATCH = 4
SEQ = 2048
DEPTH = 2
DEC_BATCH = 128
DEC_SEQ = 1
PAST_LEN = 16384
PAGE_SIZE = 128

N_EVEN = (DEPTH + 1) // 2
N_ODD = DEPTH // 2
D_A = D_MODEL // 2
CONV_A_WIDTH = 31
D_B = D_MODEL // 2
CONV_B_WIDTH = 3
IN_COLS = 2 * D_A + 3 * D_B
POOL_WINDOWS = (2, 4, 8, 16)
N_POOL_GROUPS = len(POOL_WINDOWS)
POOL_GROUP = D_MODEL // N_POOL_GROUPS
POOL_CTX = max(POOL_WINDOWS) - 1
N_MEM = 256
MEM_HEADS = 4
MEM_HEAD_DIM = D_MODEL // MEM_HEADS
PEER_HEADS = 8
PEER_KEYS = 128
PEER_EXPERTS = PEER_KEYS * PEER_KEYS
PEER_TOPK = 16
PEER_DK = 256
PEER_HALF = PEER_DK // 2
PEER_BLOCK = 128
NORM_EPS = 1e-6
LN_EPS = 1e-5

kernel_name = 'hybrid_conv_pool_peer_decoder_step'


def rmsnorm(x, g):
    xf = x.astype(jnp.float32)
    y = xf * lax.rsqrt(jnp.mean(xf * xf, axis=-1, keepdims=True) + NORM_EPS)
    return (y * g.astype(jnp.float32)).astype(x.dtype)


def layernorm(x, g, b):
    xf = x.astype(jnp.float32)
    mu = jnp.mean(xf, axis=-1, keepdims=True)
    var = jnp.mean(jnp.square(xf - mu), axis=-1, keepdims=True)
    y = (xf - mu) * lax.rsqrt(var + LN_EPS)
    return (y * g.astype(jnp.float32) + b.astype(jnp.float32)).astype(x.dtype)


def causal_depthwise(x_ext, w):
    c = x_ext.shape[-1]
    return lax.conv_general_dilated(
        x_ext, w[:, None, :].astype(x_ext.dtype), window_strides=(1,), padding='VALID',
        dimension_numbers=('NWC', 'WIO', 'NWC'), feature_group_count=c)


def conv_mixers(h, ctx_a, ctx_b, w_in, conv_a_w, conv_a_b, ln_a_g, ln_a_b, conv_b_w, w_out):
    p = h @ w_in
    a_val, a_gate, b_gate, c_gate, b_x = jnp.split(
        p, [D_A, 2 * D_A, 2 * D_A + D_B, 2 * D_A + 2 * D_B], axis=-1)
    a_in = a_val * jax.nn.sigmoid(a_gate)
    a_ext = jnp.concatenate([ctx_a, a_in], axis=1)
    a = causal_depthwise(a_ext, conv_a_w) + conv_a_b
    a = jax.nn.silu(layernorm(a, ln_a_g, ln_a_b))
    b_in = c_gate * b_x
    b_ext = jnp.concatenate([ctx_b, b_in], axis=1)
    b = b_gate * causal_depthwise(b_ext, conv_b_w)
    y = jnp.concatenate([a, b], axis=-1) @ w_out
    return y, a_ext[:, -(CONV_A_WIDTH - 1):], b_ext[:, -(CONV_B_WIDTH - 1):]


def pool_mixer(h, ctx, start_pos, w_grp, scale):
    t = h.shape[1]
    ext = jnp.concatenate([ctx, h], axis=1)
    cs = jnp.cumsum(ext.astype(jnp.float32), axis=1)
    cs = jnp.pad(cs, ((0, 0), (1, 0), (0, 0)))
    pos = start_pos + jnp.arange(t)
    outs = []
    for g, w in enumerate(POOL_WINDOWS):
        lo_c, hi_c = g * POOL_GROUP, (g + 1) * POOL_GROUP
        hi = cs[:, POOL_CTX + 1:POOL_CTX + 1 + t, lo_c:hi_c]
        lo = cs[:, POOL_CTX + 1 - w:POOL_CTX + 1 - w + t, lo_c:hi_c]
        cnt = jnp.minimum(w, pos + 1).astype(jnp.float32)[None, :, None]
        d = ((hi - lo) / cnt - h[..., lo_c:hi_c].astype(jnp.float32)).astype(h.dtype)
        outs.append(d @ w_grp[g])
    y = jnp.concatenate(outs, axis=-1) * scale
    return y, ext[:, -POOL_CTX:]


def mem_kv(mem, g, wk, wv):
    m = rmsnorm(mem, g)
    b, n, _ = m.shape
    k = (m @ wk).reshape(b, n, MEM_HEADS, MEM_HEAD_DIM)
    v = (m @ wv).reshape(b, n, MEM_HEADS, MEM_HEAD_DIM)
    return k, v


def mem_attend(h, k, v, wq, wo):
    b, t, _ = h.shape
    q = (h @ wq).reshape(b, t, MEM_HEADS, MEM_HEAD_DIM)
    s = jnp.einsum('bthd,bmhd->bhtm', q, k).astype(jnp.float32) * (MEM_HEAD_DIM ** -0.5)
    p = jax.nn.softmax(s, axis=-1).astype(v.dtype)
    o = jnp.einsum('bhtm,bmhd->bthd', p, v).reshape(b, t, MEM_HEADS * MEM_HEAD_DIM)
    return o @ wo


def peer_block(z, w_q, sub_keys, u_tab, v_tab):
    n = z.shape[0]
    q = (z @ w_q).reshape(n, PEER_HEADS, 2, PEER_HALF)
    s = jnp.einsum('nhpk,hpek->nhpe', q, sub_keys).astype(jnp.float32)
    sv, si = lax.top_k(s, PEER_TOPK)
    cand = (sv[:, :, 0, :, None] + sv[:, :, 1, None, :]).reshape(n, PEER_HEADS, PEER_TOPK * PEER_TOPK)
    cidx = (si[:, :, 0, :, None] * PEER_KEYS + si[:, :, 1, None, :]).reshape(n, PEER_HEADS, PEER_TOPK * PEER_TOPK)
    fv, fi = lax.top_k(cand, PEER_TOPK)
    experts = jnp.take_along_axis(cidx, fi, axis=-1).reshape(n, PEER_HEADS * PEER_TOPK)
    gates = jax.nn.softmax(fv, axis=-1).reshape(n, PEER_HEADS * PEER_TOPK)
    u = u_tab[experts]
    act = jax.nn.gelu(jnp.einsum('nd,nkd->nk', z, u).astype(jnp.float32)) * gates
    return jnp.einsum('nk,nkd->nd', act.astype(z.dtype), v_tab[experts])


def peer(h, w_q, sub_keys, u_tab, v_tab):
    b, t, d = h.shape
    n = b * t
    nb = -(-n // PEER_BLOCK)
    z = jnp.pad(h.reshape(n, d), ((0, nb * PEER_BLOCK - n), (0, 0))).reshape(nb, PEER_BLOCK, d)
    out = lax.map(lambda zb: peer_block(zb, w_q, sub_keys, u_tab, v_tab), z)
    return out.reshape(nb * PEER_BLOCK, d)[:n].reshape(b, t, d)


def run_group(x, mem_k, mem_v, ctx_a, ctx_b, ctx_pool, start_pos,
              norm_mix_g, norm_mem_g, norm_ffn_g, final_norm_g,
              w_in, conv_a_w, conv_a_b, ln_a_g, ln_a_b, conv_b_w, w_out,
              pool_w, pool_scale, mem_wq, mem_wo,
              peer_wq, peer_keys, peer_u, peer_v):
    new_a, new_b, new_pool = [], [], []
    for layer in range(DEPTH):
        h = rmsnorm(x, norm_mix_g[layer])
        if layer % 2 == 0:
            e = layer // 2
            y, ca, cb = conv_mixers(h, ctx_a[e], ctx_b[e], w_in[e], conv_a_w[e], conv_a_b[e],
                                    ln_a_g[e], ln_a_b[e], conv_b_w[e], w_out[e])
            new_a.append(ca)
            new_b.append(cb)
        else:
            o = layer // 2
            y, cp = pool_mixer(h, ctx_pool[o], start_pos, pool_w[o], pool_scale[o])
            new_pool.append(cp)
        x = x + y
        x = x + mem_attend(rmsnorm(x, norm_mem_g[layer]), mem_k[layer], mem_v[layer],
                           mem_wq[layer], mem_wo[layer])
        x = x + peer(rmsnorm(x, norm_ffn_g[layer]), peer_wq[layer], peer_keys[layer],
                     peer_u[layer], peer_v[layer])
    return rmsnorm(x, final_norm_g), jnp.stack(new_a), jnp.stack(new_b), jnp.stack(new_pool)


def setup_inputs(seed: int = 0) -> dict:
    key = jax.random.key(seed)
    ks = iter(jax.random.split(key, 40))

    def nrm(shape, scale):
        return jax.random.normal(next(ks), shape, jnp.float32) * scale

    def gain(shape):
        return 1.0 + nrm(shape, 0.02)

    d = D_MODEL
    return {
        'x_prompt': nrm((BATCH, SEQ, d), 1.0),
        'x_sample': nrm((DEC_BATCH, DEC_SEQ, d), 1.0),
        'mem_prompt': nrm((BATCH, N_MEM, d), 1.0),
        'cache_mem_k': nrm((DEPTH, DEC_BATCH, N_MEM, MEM_HEADS, MEM_HEAD_DIM), 1.0),
        'cache_mem_v': nrm((DEPTH, DEC_BATCH, N_MEM, MEM_HEADS, MEM_HEAD_DIM), 1.0),
        'state_conv_a': nrm((N_EVEN, DEC_BATCH, CONV_A_WIDTH - 1, D_A), 0.5),
        'state_conv_b': nrm((N_EVEN, DEC_BATCH, CONV_B_WIDTH - 1, D_B), 0.5),
        'state_pool': nrm((N_ODD, DEC_BATCH, POOL_CTX, d), 1.0),
        'norm_mix_g': gain((DEPTH, d)),
        'norm_mem_g': gain((DEPTH, d)),
        'norm_ffn_g': gain((DEPTH, d)),
        'mem_norm_g': gain((DEPTH, d)),
        'final_norm_g': gain((d,)),
        'w_in': nrm((N_EVEN, d, IN_COLS), d ** -0.5),
        'conv_a_w': nrm((N_EVEN, CONV_A_WIDTH, D_A), CONV_A_WIDTH ** -0.5),
        'conv_a_b': nrm((N_EVEN, D_A), 0.02),
        'ln_a_g': gain((N_EVEN, D_A)),
        'ln_a_b': nrm((N_EVEN, D_A), 0.02),
        'conv_b_w': nrm((N_EVEN, CONV_B_WIDTH, D_B), CONV_B_WIDTH ** -0.5),
        'w_out': nrm((N_EVEN, D_A + D_B, d), (D_A + D_B) ** -0.5),
        'pool_w': nrm((N_ODD, N_POOL_GROUPS, POOL_GROUP, POOL_GROUP), POOL_GROUP ** -0.5),
        'pool_scale': gain((N_ODD, d)),
        'mem_wq': nrm((DEPTH, d, d), d ** -0.5),
        'mem_wk': nrm((DEPTH, d, d), d ** -0.5),
        'mem_wv': nrm((DEPTH, d, d), d ** -0.5),
        'mem_wo': nrm((DEPTH, d, d), d ** -0.5),
        'peer_wq': nrm((DEPTH, d, PEER_HEADS * PEER_DK), d ** -0.5),
        'peer_keys': nrm((DEPTH, PEER_HEADS, 2, PEER_KEYS, PEER_HALF), PEER_HALF ** -0.5),
        'peer_u': nrm((DEPTH, PEER_EXPERTS, d), d ** -0.5),
        'peer_v': nrm((DEPTH, PEER_EXPERTS, d), PEER_HEADS ** -0.5),
    }


def reference(x_prompt, x_sample, mem_prompt, cache_mem_k, cache_mem_v, state_conv_a, state_conv_b, state_pool,
              norm_mix_g, norm_mem_g, norm_ffn_g, mem_norm_g, final_norm_g,
              w_in, conv_a_w, conv_a_b, ln_a_g, ln_a_b, conv_b_w, w_out,
              pool_w, pool_scale, mem_wq, mem_wk, mem_wv, mem_wo,
              peer_wq, peer_keys, peer_u, peer_v):
    shared = (norm_mix_g, norm_mem_g, norm_ffn_g, final_norm_g,
              w_in, conv_a_w, conv_a_b, ln_a_g, ln_a_b, conv_b_w, w_out,
              pool_w, pool_scale, mem_wq, mem_wo, peer_wq, peer_keys, peer_u, peer_v)
    kv = [mem_kv(mem_prompt, mem_norm_g[l], mem_wk[l], mem_wv[l]) for l in range(DEPTH)]
    prompt_mem_k = jnp.stack([k for k, _ in kv])
    prompt_mem_v = jnp.stack([v for _, v in kv])
    dt = x_prompt.dtype
    zero_a = jnp.zeros((N_EVEN, BATCH, CONV_A_WIDTH - 1, D_A), dt)
    zero_b = jnp.zeros((N_EVEN, BATCH, CONV_B_WIDTH - 1, D_B), dt)
    zero_p = jnp.zeros((N_ODD, BATCH, POOL_CTX, D_MODEL), dt)
    y_prompt, prompt_conv_a, prompt_conv_b, prompt_pool = run_group(
        x_prompt, prompt_mem_k, prompt_mem_v, zero_a, zero_b, zero_p, 0, *shared)
    y_sample, sample_conv_a, sample_conv_b, sample_pool = run_group(
        x_sample, cache_mem_k, cache_mem_v, state_conv_a, state_conv_b, state_pool, PAST_LEN, *shared)
    return (y_prompt, y_sample, prompt_mem_k, prompt_mem_v, prompt_conv_a, prompt_conv_b, prompt_pool,
            sample_conv_a, sample_conv_b, sample_pool)
```

```python
import functools

import jax
import jax.numpy as jnp
from jax import lax
from jax.experimental import pallas as pl
from jax.experimental.pallas import tpu as pltpu

F32 = jnp.float32
BF16 = jnp.bfloat16

D_MODEL = 2048
BATCH = 4
SEQ = 2048
DEPTH = 2
DEC_BATCH = 128
N_PROMPT = BATCH * SEQ
N_TOK = N_PROMPT + DEC_BATCH
D_A = D_MODEL // 2
D_B = D_MODEL // 2
CONV_A_WIDTH = 31
CONV_B_WIDTH = 3
IN_COLS = 2 * D_A + 3 * D_B
POOL_WINDOWS = (2, 4, 8, 16)
POOL_GROUP = D_MODEL // len(POOL_WINDOWS)
POOL_CTX = max(POOL_WINDOWS) - 1
N_MEM = 256
MEM_HEADS = 4
MEM_HEAD_DIM = D_MODEL // MEM_HEADS
PEER_HEADS = 8
PEER_KEYS = 128
PEER_TOPK = 16
PEER_HALF = 128
NORM_EPS = 1e-6
LN_EPS = 1e-5

V7X_LANES = 128
V7X_SUBLANES = 8
V7X_VMEM_BYTES = 64 * 1024 * 1024

TOK_TILE = 640
LIN_TN = 512
CONV_TQ = 256
CONV_HALO = 32
CONV_ROWS = 32
POOL_TQ = 256
POOL_HALO = 16
ATT_TQ = 512
ATT_BB = 4
SAMPLE_BB = 32
ROUTE_TT = 128
PEER_CI = 4
PEER_EC = PEER_CI * PEER_KEYS

assert N_TOK % TOK_TILE == 0 and N_TOK % ROUTE_TT == 0


def _params(sem, vmem_mb):
    return pltpu.CompilerParams(dimension_semantics=sem, vmem_limit_bytes=vmem_mb * 1024 * 1024)


def _rms(x, g):
    return x * lax.rsqrt(jnp.mean(x * x, axis=-1, keepdims=True) + NORM_EPS) * g


def _linear_kernel(*refs, norm, has_res, emit_h):
    it = iter(refs)
    x_ref = next(it)
    g_ref = next(it) if norm else None
    w_ref = next(it)
    res_ref = next(it) if has_res else None
    o_ref = next(it)
    h_out_ref = next(it) if emit_h else None
    h_scr = next(it) if norm else None

    if norm:
        @pl.when(pl.program_id(1) == 0)
        def _():
            h = _rms(x_ref[...], g_ref[...]).astype(BF16)
            h_scr[...] = h
            if emit_h:
                h_out_ref[...] = h
        h = h_scr[...]
    else:
        h = x_ref[...].astype(BF16)
    y = jnp.dot(h, w_ref[...], preferred_element_type=F32)
    if has_res:
        y = res_ref[...] + y
    o_ref[...] = y


def _linear(x, w, g=None, res=None, emit_h=False, rows=None, row_off=0, tm=TOK_TILE, tn=LIN_TN):
    k, dout = w.shape
    rows = x.shape[0] if rows is None else rows
    assert rows % tm == 0 and dout % tn == 0 and row_off % tm == 0
    ro = row_off // tm
    norm = g is not None
    in_specs = [pl.BlockSpec((tm, k), lambda i, j: (i + ro, 0))]
    args = [x]
    if norm:
        in_specs.append(pl.BlockSpec((1, k), lambda i, j: (0, 0)))
        args.append(g)
    in_specs.append(pl.BlockSpec((k, tn), lambda i, j: (0, j)))
    args.append(w)
    if res is not None:
        in_specs.append(pl.BlockSpec((tm, tn), lambda i, j: (i + ro, j)))
        args.append(res)
    out_shape = [jax.ShapeDtypeStruct((rows, dout), F32)]
    out_specs = [pl.BlockSpec((tm, tn), lambda i, j: (i, j))]
    if emit_h:
        out_shape.append(jax.ShapeDtypeStruct((rows, k), BF16))
        out_specs.append(pl.BlockSpec((tm, k), lambda i, j: (i, 0)))
    outs = pl.pallas_call(
        functools.partial(_linear_kernel, norm=norm, has_res=res is not None, emit_h=emit_h),
        grid=(rows // tm, dout // tn),
        in_specs=in_specs,
        out_specs=out_specs,
        out_shape=out_shape,
        scratch_shapes=[pltpu.VMEM((tm, k), BF16)] if norm else [],
        compiler_params=_params(("parallel", "arbitrary"), 48),
        name="linear",
    )(*args)
    return outs if emit_h else outs[0]


def _rmsnorm_kernel(x_ref, g_ref, o_ref):
    o_ref[...] = _rms(x_ref[...], g_ref[...])


def _rmsnorm(x, g, rows, row_off, tm):
    ro = row_off // tm
    return pl.pallas_call(
        _rmsnorm_kernel,
        grid=(rows // tm,),
        in_specs=[pl.BlockSpec((tm, D_MODEL), lambda i: (i + ro, 0)),
                  pl.BlockSpec((1, D_MODEL), lambda i: (0, 0))],
        out_specs=pl.BlockSpec((tm, D_MODEL), lambda i: (i, 0)),
        out_shape=jax.ShapeDtypeStruct((rows, D_MODEL), F32),
        compiler_params=_params(("parallel",), 32),
        name="final_norm",
    )(x, g)


def _layernorm_silu(a, g, b):
    mu = jnp.mean(a, axis=-1, keepdims=True)
    ac = a - mu
    var = jnp.mean(ac * ac, axis=-1, keepdims=True)
    y = ac * lax.rsqrt(var + LN_EPS) * g + b
    return y * jax.nn.sigmoid(y)


def _conv_prompt_kernel(av_ref, ag_ref, bg_ref, cg_ref, bx_ref,
                        hav_ref, hag_ref, hcg_ref, hbx_ref,
                        wa_ref, ba_ref, lg_ref, lb_ref, wb_ref,
                        ab_ref, atail_ref, btail_ref,
                        aext, bext, aconv):
    i = pl.program_id(1)
    tq, halo = CONV_TQ, CONV_HALO
    keep = (i > 0).astype(F32)
    a_in = av_ref[...] * jax.nn.sigmoid(ag_ref[...])
    b_in = cg_ref[...] * bx_ref[...]
    aext[0:halo, :] = hav_ref[...] * jax.nn.sigmoid(hag_ref[...]) * keep
    aext[halo:halo + tq, :] = a_in
    bext[0:halo, :] = hcg_ref[...] * hbx_ref[...] * keep
    bext[halo:halo + tq, :] = b_in

    off_a = halo - (CONV_A_WIDTH - 1)
    for r0 in range(0, tq, CONV_ROWS):
        acc = jnp.broadcast_to(ba_ref[...], (CONV_ROWS, D_A))
        for k in range(CONV_A_WIDTH):
            acc = acc + wa_ref[k:k + 1, :] * aext[r0 + off_a + k:r0 + off_a + k + CONV_ROWS, :]
        aconv[r0:r0 + CONV_ROWS, :] = acc
    a = _layernorm_silu(aconv[...], lg_ref[...], lb_ref[...])

    off_b = halo - (CONV_B_WIDTH - 1)
    cb = wb_ref[0:1, :] * bext[off_b:off_b + tq, :]
    for k in range(1, CONV_B_WIDTH):
        cb = cb + wb_ref[k:k + 1, :] * bext[off_b + k:off_b + k + tq, :]
    b = bg_ref[...] * cb

    ab_ref[:, 0:D_A] = a.astype(BF16)
    ab_ref[:, D_A:D_A + D_B] = b.astype(BF16)

    @pl.when(i == pl.num_programs(1) - 1)
    def _():
        atail_ref[...] = a_in[tq - CONV_HALO:, :]
        btail_ref[...] = b_in[tq - V7X_SUBLANES:, :]


def _conv_prompt(p, wa, ba, lg, lb, wb):
    tq, halo = CONV_TQ, CONV_HALO
    nt = SEQ // tq
    cur = lambda c: pl.BlockSpec((tq, D_A), lambda b, i, c=c: (b * nt + i, c))
    hal = lambda c: pl.BlockSpec(
        (halo, D_A), lambda b, i, c=c: (jnp.maximum((b * SEQ + i * tq) // halo - 1, 0), c))
    full = lambda r: pl.BlockSpec((r, D_A), lambda b, i: (0, 0))
    return pl.pallas_call(
        _conv_prompt_kernel,
        grid=(BATCH, nt),
        in_specs=[cur(0), cur(1), cur(2), cur(3), cur(4), hal(0), hal(1), hal(3), hal(4),
                  full(CONV_A_WIDTH), full(1), full(1), full(1), full(CONV_B_WIDTH)],
        out_specs=[pl.BlockSpec((tq, D_MODEL), lambda b, i: (b * nt + i, 0)),
                   pl.BlockSpec((None, CONV_HALO, D_A), lambda b, i: (b, 0, 0)),
                   pl.BlockSpec((None, V7X_SUBLANES, D_B), lambda b, i: (b, 0, 0))],
        out_shape=[jax.ShapeDtypeStruct((N_PROMPT, D_MODEL), BF16),
                   jax.ShapeDtypeStruct((BATCH, CONV_HALO, D_A), F32),
                   jax.ShapeDtypeStruct((BATCH, V7X_SUBLANES, D_B), F32)],
        scratch_shapes=[pltpu.VMEM((halo + tq, D_A), F32), pltpu.VMEM((halo + tq, D_B), F32),
                        pltpu.VMEM((tq, D_A), F32)],
        compiler_params=_params(("parallel", "arbitrary"), 48),
        name="conv_prompt",
    )(p, p, p, p, p, p, p, p, p, wa, ba, lg, lb, wb)


def _conv_sample_kernel(av_ref, ag_ref, bg_ref, cg_ref, bx_ref, sa_ref, sb_ref,
                        wa_ref, ba_ref, lg_ref, lb_ref, wb_ref,
                        ab_ref, ain_ref, bin_ref):
    na, nb = CONV_A_WIDTH - 1, CONV_B_WIDTH - 1
    a_in = av_ref[...] * jax.nn.sigmoid(ag_ref[...])
    b_in = cg_ref[...] * bx_ref[...]
    conv_a = jnp.sum(sa_ref[...] * wa_ref[0:na, :][None], axis=1) + wa_ref[na:na + 1, :] * a_in
    a = _layernorm_silu(conv_a + ba_ref[...], lg_ref[...], lb_ref[...])
    conv_b = jnp.sum(sb_ref[...] * wb_ref[0:nb, :][None], axis=1) + wb_ref[nb:nb + 1, :] * b_in
    b = bg_ref[...] * conv_b
    ab_ref[:, 0:D_A] = a.astype(BF16)
    ab_ref[:, D_A:D_A + D_B] = b.astype(BF16)
    ain_ref[...] = a_in
    bin_ref[...] = b_in


def _conv_sample(p, state_a, state_b, wa, ba, lg, lb, wb):
    bb = SAMPLE_BB
    r0 = N_PROMPT // bb
    cur = lambda c: pl.BlockSpec((bb, D_A), lambda i, c=c: (r0 + i, c))
    full = lambda r: pl.BlockSpec((r, D_A), lambda i: (0, 0))
    return pl.pallas_call(
        _conv_sample_kernel,
        grid=(DEC_BATCH // bb,),
        in_specs=[cur(0), cur(1), cur(2), cur(3), cur(4),
                  pl.BlockSpec((bb, CONV_A_WIDTH - 1, D_A), lambda i: (i, 0, 0)),
                  pl.BlockSpec((bb, CONV_B_WIDTH - 1, D_B), lambda i: (i, 0, 0)),
                  full(CONV_A_WIDTH), full(1), full(1), full(1), full(CONV_B_WIDTH)],
        out_specs=[pl.BlockSpec((bb, D_MODEL), lambda i: (i, 0)),
                   pl.BlockSpec((bb, D_A), lambda i: (i, 0)),
                   pl.BlockSpec((bb, D_B), lambda i: (i, 0))],
        out_shape=[jax.ShapeDtypeStruct((DEC_BATCH, D_MODEL), BF16),
                   jax.ShapeDtypeStruct((DEC_BATCH, D_A), F32),
                   jax.ShapeDtypeStruct((DEC_BATCH, D_B), F32)],
        compiler_params=_params(("parallel",), 32),
        name="conv_sample",
    )(p, p, p, p, p, state_a, state_b, wa, ba, lg, lb, wb)


def _pool_prompt_kernel(x_ref, hx_ref, g_ref, d_ref, tail_ref, ext):
    i = pl.program_id(1)
    tq, halo = POOL_TQ, POOL_HALO
    keep = (i > 0).astype(F32)
    h = _rms(x_ref[...], g_ref[...])
    ext[0:halo, :] = _rms(hx_ref[...], g_ref[...]) * keep
    ext[halo:halo + tq, :] = h
    pos = (i * tq + lax.broadcasted_iota(jnp.int32, (tq, 1), 0)).astype(F32)
    for gi, w in enumerate(POOL_WINDOWS):
        c0 = gi * POOL_GROUP
        s = h[:, c0:c0 + POOL_GROUP]
        for k in range(1, w):
            s = s + ext[halo - k:halo - k + tq, c0:c0 + POOL_GROUP]
        cnt = jnp.minimum(float(w), pos + 1.0)
        d_ref[:, c0:c0 + POOL_GROUP] = (s / cnt - h[:, c0:c0 + POOL_GROUP]).astype(BF16)

    @pl.when(i == pl.num_programs(1) - 1)
    def _():
        tail_ref[...] = h[tq - POOL_HALO:, :]


def _pool_prompt(x, g):
    tq, halo = POOL_TQ, POOL_HALO
    nt = SEQ // tq
    return pl.pallas_call(
        _pool_prompt_kernel,
        grid=(BATCH, nt),
        in_specs=[pl.BlockSpec((tq, D_MODEL), lambda b, i: (b * nt + i, 0)),
                  pl.BlockSpec((halo, D_MODEL),
                               lambda b, i: (jnp.maximum((b * SEQ + i * tq) // halo - 1, 0), 0)),
                  pl.BlockSpec((1, D_MODEL), lambda b, i: (0, 0))],
        out_specs=[pl.BlockSpec((tq, D_MODEL), lambda b, i: (b * nt + i, 0)),
                   pl.BlockSpec((None, POOL_HALO, D_MODEL), lambda b, i: (b, 0, 0))],
        out_shape=[jax.ShapeDtypeStruct((N_PROMPT, D_MODEL), BF16),
                   jax.ShapeDtypeStruct((BATCH, POOL_HALO, D_MODEL), F32)],
        scratch_shapes=[pltpu.VMEM((halo + tq, D_MODEL), F32)],
        compiler_params=_params(("parallel", "arbitrary"), 48),
        name="pool_prompt",
    )(x, x, g)


def _pool_sample_kernel(x_ref, st_ref, g_ref, d_ref, h_ref):
    h = _rms(x_ref[...], g_ref[...])
    h_ref[...] = h
    for gi, w in enumerate(POOL_WINDOWS):
        c0 = gi * POOL_GROUP
        s = h[:, c0:c0 + POOL_GROUP] + jnp.sum(
            st_ref[:, POOL_CTX - (w - 1):POOL_CTX, c0:c0 + POOL_GROUP], axis=1)
        d_ref[:, c0:c0 + POOL_GROUP] = (s / float(w) - h[:, c0:c0 + POOL_GROUP]).astype(BF16)


def _pool_sample(x, state, g):
    bb = SAMPLE_BB
    r0 = N_PROMPT // bb
    return pl.pallas_call(
        _pool_sample_kernel,
        grid=(DEC_BATCH // bb,),
        in_specs=[pl.BlockSpec((bb, D_MODEL), lambda i: (r0 + i, 0)),
                  pl.BlockSpec((bb, POOL_CTX, D_MODEL), lambda i: (i, 0, 0)),
                  pl.BlockSpec((1, D_MODEL), lambda i: (0, 0))],
        out_specs=[pl.BlockSpec((bb, D_MODEL), lambda i: (i, 0)),
                   pl.BlockSpec((bb, D_MODEL), lambda i: (i, 0))],
        out_shape=[jax.ShapeDtypeStruct((DEC_BATCH, D_MODEL), BF16),
                   jax.ShapeDtypeStruct((DEC_BATCH, D_MODEL), F32)],
        compiler_params=_params(("parallel",), 32),
        name="pool_sample",
    )(x, state, g)


def _pool_proj_kernel(d_ref, w_ref, sc_ref, res_ref, o_ref):
    y = jnp.dot(d_ref[...], w_ref[...], preferred_element_type=F32)
    o_ref[...] = res_ref[...] + y * sc_ref[...]


def _pool_proj(d, w, scale, res, tm=TOK_TILE):
    pg = POOL_GROUP
    return pl.pallas_call(
        _pool_proj_kernel,
        grid=(N_TOK // tm, len(POOL_WINDOWS)),
        in_specs=[pl.BlockSpec((tm, pg), lambda i, j: (i, j)),
                  pl.BlockSpec((None, pg, pg), lambda i, j: (j, 0, 0)),
                  pl.BlockSpec((1, pg), lambda i, j: (0, j)),
                  pl.BlockSpec((tm, pg), lambda i, j: (i, j))],
        out_specs=pl.BlockSpec((tm, pg), lambda i, j: (i, j)),
        out_shape=jax.ShapeDtypeStruct((N_TOK, D_MODEL), F32),
        compiler_params=_params(("parallel", "arbitrary"), 32),
        name="pool_proj",
    )(d, w, scale, res)


def _attn_prompt_kernel(q_ref, k_ref, v_ref, o_ref):
    hd = MEM_HEAD_DIM
    for h in range(MEM_HEADS):
        q = q_ref[:, h * hd:(h + 1) * hd].astype(BF16)
        k = k_ref[:, h * hd:(h + 1) * hd].astype(BF16)
        v = v_ref[:, h * hd:(h + 1) * hd].astype(BF16)
        s = lax.dot_general(q, k, (((1,), (1,)), ((), ())), preferred_element_type=F32) * hd ** -0.5
        e = jnp.exp(s - jnp.max(s, axis=-1, keepdims=True))
        p = (e / jnp.sum(e, axis=-1, keepdims=True)).astype(BF16)
        o_ref[:, h * hd:(h + 1) * hd] = jnp.dot(p, v, preferred_element_type=F32).astype(BF16)


def _attn_prompt(q, k, v):
    tq = ATT_TQ
    nt = SEQ // tq
    return pl.pallas_call(
        _attn_prompt_kernel,
        grid=(BATCH, nt),
        in_specs=[pl.BlockSpec((tq, D_MODEL), lambda b, i: (b * nt + i, 0)),
                  pl.BlockSpec((None, N_MEM, D_MODEL), lambda b, i: (b, 0, 0)),
                  pl.BlockSpec((None, N_MEM, D_MODEL), lambda b, i: (b, 0, 0))],
        out_specs=pl.BlockSpec((tq, D_MODEL), lambda b, i: (b * nt + i, 0)),
        out_shape=jax.ShapeDtypeStruct((N_PROMPT, D_MODEL), BF16),
        compiler_params=_params(("parallel", "arbitrary"), 32),
        name="attn_prompt",
    )(q, k, v)


def _attn_sample_kernel(q_ref, k_ref, v_ref, o_ref):
    hd = MEM_HEAD_DIM
    for b in range(ATT_BB):
        prod = k_ref[b] * q_ref[b]
        for h in range(MEM_HEADS):
            s = jnp.sum(prod[:, h * hd:(h + 1) * hd], axis=-1, keepdims=True) * hd ** -0.5
            e = jnp.exp(s - jnp.max(s, axis=0, keepdims=True))
            p = e / jnp.sum(e, axis=0, keepdims=True)
            o = jnp.sum(p * v_ref[b, :, h * hd:(h + 1) * hd], axis=0, keepdims=True)
            o_ref[b, :, h * hd:(h + 1) * hd] = o.astype(BF16)


def _attn_sample(q, k, v):
    bb = ATT_BB
    return pl.pallas_call(
        _attn_sample_kernel,
        grid=(DEC_BATCH // bb,),
        in_specs=[pl.BlockSpec((bb, 1, D_MODEL), lambda i: (i, 0, 0)),
                  pl.BlockSpec((bb, N_MEM, D_MODEL), lambda i: (i, 0, 0)),
                  pl.BlockSpec((bb, N_MEM, D_MODEL), lambda i: (i, 0, 0))],
        out_specs=pl.BlockSpec((bb, 1, D_MODEL), lambda i: (i, 0, 0)),
        out_shape=jax.ShapeDtypeStruct((DEC_BATCH, 1, D_MODEL), BF16),
        compiler_params=_params(("parallel",), 48),
        name="attn_sample",
    )(q, k, v)


def _top16(s, iota):
    n = s.shape[0]
    work = s
    rank = jnp.full(s.shape, float(PEER_TOPK), F32)
    vals = []
    for a in range(PEER_TOPK):
        m = jnp.max(work, axis=0, keepdims=True)
        first = jnp.min(jnp.where(work == m, iota, float(n)), axis=0, keepdims=True)
        sel = iota == first
        rank = jnp.where(sel, float(a), rank)
        work = jnp.where(sel, -jnp.inf, work)
        vals.append(m)
    return rank, vals


def _route_kernel(q_ref, keys_ref, p_ref, t0_ref, qq_ref, r1_ref):
    tt = ROUTE_TT
    iota_k = lax.broadcasted_iota(jnp.int32, (PEER_KEYS, tt), 0).astype(F32)
    iota_c = lax.broadcasted_iota(jnp.int32, (PEER_TOPK * PEER_TOPK, tt), 0).astype(F32)
    for h in range(PEER_HEADS):
        scores, ranks, svals = [], [], []
        for part in range(2):
            c0 = (2 * h + part) * PEER_HALF
            qh = q_ref[:, c0:c0 + PEER_HALF].astype(BF16)
            s = lax.dot_general(keys_ref[h, part], qh, (((1,), (1,)), ((), ())),
                                preferred_element_type=F32)
            rank, vals = _top16(s, iota_k)
            scores.append(s)
            ranks.append(rank)
            svals.append(vals)
        sv1 = jnp.concatenate(svals[1], axis=0)
        cand = jnp.concatenate([svals[0][a] + sv1 for a in range(PEER_TOPK)], axis=0)
        crank, _ = _top16(cand, iota_c)
        selm = crank < float(PEER_TOPK)
        z = jnp.sum(jnp.where(selm, jnp.exp(cand - cand[0:1, :]), 0.0), axis=0, keepdims=True)
        e0 = jnp.where(ranks[0] < float(PEER_TOPK), jnp.exp(scores[0] - svals[0][0]), 0.0)
        e1 = jnp.where(ranks[1] < float(PEER_TOPK), jnp.exp(scores[1] - svals[1][0]), 0.0)
        t0 = jnp.zeros((PEER_KEYS, tt), F32)
        for a in range(PEER_TOPK):
            cnt = jnp.sum(selm[a * PEER_TOPK:(a + 1) * PEER_TOPK, :].astype(F32), axis=0, keepdims=True)
            t0 = jnp.where(ranks[0] == float(a), cnt, t0)
        p_ref[h] = e0 / z
        t0_ref[h] = t0
        qq_ref[h] = e1
        r1_ref[h] = ranks[1]


def _route(q, keys):
    tt = ROUTE_TT
    blk = pl.BlockSpec((PEER_HEADS, PEER_KEYS, tt), lambda i: (0, 0, i))
    shp = jax.ShapeDtypeStruct((PEER_HEADS, PEER_KEYS, N_TOK), F32)
    return pl.pallas_call(
        _route_kernel,
        grid=(N_TOK // tt,),
        in_specs=[pl.BlockSpec((tt, D_MODEL), lambda i: (i, 0)),
                  pl.BlockSpec((PEER_HEADS, 2, PEER_KEYS, PEER_HALF), lambda i: (0, 0, 0, 0))],
        out_specs=[blk, blk, blk, blk],
        out_shape=[shp, shp, shp, shp],
        compiler_params=_params(("parallel",), 32),
        name="peer_route",
    )(q, keys)


def _peer_dense_kernel(z_ref, u_ref, v_ref, p_ref, t0_ref, qq_ref, r1_ref, res_ref, o_ref):
    c = pl.program_id(1)

    @pl.when(c == 0)
    def _():
        o_ref[...] = res_ref[...]

    s_t = lax.dot_general(u_ref[...], z_ref[...], (((1,), (1,)), ((), ())),
                          preferred_element_type=F32)
    acts = []
    for il in range(PEER_CI):
        g = None
        for h in range(PEER_HEADS):
            prow = p_ref[il, h:h + 1, :]
            trow = t0_ref[il, h:h + 1, :]
            gh = jnp.where(r1_ref[h] < trow, qq_ref[h] * prow, 0.0)
            g = gh if g is None else g + gh
        s = s_t[il * PEER_KEYS:(il + 1) * PEER_KEYS, :]
        acts.append((jax.nn.gelu(s, approximate=True) * g).astype(BF16))
    act = jnp.concatenate(acts, axis=0)
    o_ref[...] += lax.dot_general(act, v_ref[...], (((0,), (0,)), ((), ())),
                                  preferred_element_type=F32)


def _peer_dense(z, u, v, p, t0, qq, r1, res):
    tt, ec = TOK_TILE, PEER_EC
    n_chunks = u.shape[0] // ec
    first = pl.BlockSpec((PEER_CI, PEER_HEADS, tt), lambda t, c: (c, 0, t))
    second = pl.BlockSpec((PEER_HEADS, PEER_KEYS, tt), lambda t, c: (0, 0, t))
    return pl.pallas_call(
        _peer_dense_kernel,
        grid=(N_TOK // tt, n_chunks),
        in_specs=[pl.BlockSpec((tt, D_MODEL), lambda t, c: (t, 0)),
                  pl.BlockSpec((ec, D_MODEL), lambda t, c: (c, 0)),
                  pl.BlockSpec((ec, D_MODEL), lambda t, c: (c, 0)),
                  first, first, second, second,
                  pl.BlockSpec((tt, D_MODEL), lambda t, c: (t, 0))],
        out_specs=pl.BlockSpec((tt, D_MODEL), lambda t, c: (t, 0)),
        out_shape=jax.ShapeDtypeStruct((N_TOK, D_MODEL), F32),
        compiler_params=_params(("parallel", "arbitrary"), 56),
        name="peer_dense",
    )(z, u, v, p, t0, qq, r1, res)


def _row(v):
    return v.reshape(1, -1)


def kernel(x_prompt, x_sample, mem_prompt, cache_mem_k, cache_mem_v, state_conv_a, state_conv_b, state_pool, norm_mix_g, norm_mem_g, norm_ffn_g, mem_norm_g, final_norm_g, w_in, conv_a_w, conv_a_b, ln_a_g, ln_a_b, conv_b_w, w_out, pool_w, pool_scale, mem_wq, mem_wk, mem_wv, mem_wo, peer_wq, peer_keys, peer_u, peer_v):
    bf = lambda a: a.astype(BF16)
    x = jnp.concatenate([x_prompt.reshape(N_PROMPT, D_MODEL), x_sample.reshape(DEC_BATCH, D_MODEL)], axis=0)

    mem = mem_prompt.reshape(BATCH * N_MEM, D_MODEL)
    pk, pv = [], []
    for l in range(DEPTH):
        wkv = jnp.concatenate([bf(mem_wk[l]), bf(mem_wv[l])], axis=1)
        kv = _linear(mem, wkv, g=_row(mem_norm_g[l]), tm=BATCH * N_MEM // 2)
        pk.append(kv[:, :D_MODEL].reshape(BATCH, N_MEM, D_MODEL))
        pv.append(kv[:, D_MODEL:].reshape(BATCH, N_MEM, D_MODEL))

    new_a, new_b, new_pool = [], [], []
    for l in range(DEPTH):
        if l % 2 == 0:
            e = l // 2
            p = _linear(x, bf(w_in[e]), g=_row(norm_mix_g[l]))
            cw = (conv_a_w[e], _row(conv_a_b[e]), _row(ln_a_g[e]), _row(ln_a_b[e]), conv_b_w[e])
            ab_p, a_tail, b_tail = _conv_prompt(p, *cw)
            ab_s, a_in_s, b_in_s = _conv_sample(p, state_conv_a[e], state_conv_b[e], *cw)
            x = _linear(jnp.concatenate([ab_p, ab_s], axis=0), bf(w_out[e]), res=x)
            new_a.append((a_tail[:, CONV_HALO - (CONV_A_WIDTH - 1):],
                          jnp.concatenate([state_conv_a[e][:, 1:], a_in_s[:, None]], axis=1)))
            new_b.append((b_tail[:, V7X_SUBLANES - (CONV_B_WIDTH - 1):],
                          jnp.concatenate([state_conv_b[e][:, 1:], b_in_s[:, None]], axis=1)))
        else:
            o = l // 2
            g = _row(norm_mix_g[l])
            d_p, h_tail = _pool_prompt(x, g)
            d_s, h_s = _pool_sample(x, state_pool[o], g)
            x = _pool_proj(jnp.concatenate([d_p, d_s], axis=0), bf(pool_w[o]), _row(pool_scale[o]), x)
            new_pool.append((h_tail[:, POOL_HALO - POOL_CTX:],
                             jnp.concatenate([state_pool[o][:, 1:], h_s[:, None]], axis=1)))

        q = _linear(x, bf(mem_wq[l]), g=_row(norm_mem_g[l]))
        o_p = _attn_prompt(q, pk[l], pv[l])
        o_s = _attn_sample(q[N_PROMPT:].reshape(DEC_BATCH, 1, D_MODEL),
                           cache_mem_k[l].reshape(DEC_BATCH, N_MEM, D_MODEL),
                           cache_mem_v[l].reshape(DEC_BATCH, N_MEM, D_MODEL))
        x = _linear(jnp.concatenate([o_p, o_s.reshape(DEC_BATCH, D_MODEL)], axis=0), bf(mem_wo[l]), res=x)

        pq, z = _linear(x, bf(peer_wq[l]), g=_row(norm_ffn_g[l]), emit_h=True)
        gp, gt0, gq, gr1 = _route(pq, bf(peer_keys[l]))
        x = _peer_dense(z, bf(peer_u[l]), bf(peer_v[l]),
                        jnp.swapaxes(gp, 0, 1), jnp.swapaxes(gt0, 0, 1), gq, gr1, x)

    y_prompt = _rmsnorm(x, _row(final_norm_g), N_PROMPT, 0, 512).reshape(BATCH, SEQ, D_MODEL)
    y_sample = _rmsnorm(x, _row(final_norm_g), DEC_BATCH, N_PROMPT, DEC_BATCH).reshape(DEC_BATCH, 1, D_MODEL)
    shape_kv = (DEPTH, BATCH, N_MEM, MEM_HEADS, MEM_HEAD_DIM)
    return (y_prompt, y_sample,
            jnp.stack(pk).reshape(shape_kv), jnp.stack(pv).reshape(shape_kv),
            jnp.stack([a for a, _ in new_a]), jnp.stack([b for b, _ in new_b]),
            jnp.stack([c for c, _ in new_pool]),
            jnp.stack([a for _, a in new_a]), jnp.stack([b for _, b in new_b]),
            jnp.stack([c for _, c in new_pool]))
```

```python
import functools

import jax
import jax.numpy as jnp
from jax import lax
from jax.experimental import pallas as pl
from jax.experimental.pallas import tpu as pltpu

F32 = jnp.float32
BF16 = jnp.bfloat16

D_MODEL = 2048
BATCH = 4
SEQ = 2048
DEPTH = 2
DEC_BATCH = 128
N_PROMPT = BATCH * SEQ
N_TOK = N_PROMPT + DEC_BATCH
D_A = D_MODEL // 2
D_B = D_MODEL // 2
CONV_A_WIDTH = 31
CONV_B_WIDTH = 3
IN_COLS = 2 * D_A + 3 * D_B
POOL_WINDOWS = (2, 4, 8, 16)
POOL_GROUP = D_MODEL // len(POOL_WINDOWS)
POOL_CTX = max(POOL_WINDOWS) - 1
N_MEM = 256
MEM_HEADS = 4
MEM_HEAD_DIM = D_MODEL // MEM_HEADS
PEER_HEADS = 8
PEER_KEYS = 128
PEER_TOPK = 16
PEER_HALF = 128
NORM_EPS = 1e-6
LN_EPS = 1e-5

V7X_LANES = 128
V7X_SUBLANES = 8
V7X_VMEM_BYTES = 64 * 1024 * 1024

TOK_TILE = 640
LIN_TN = 512
CONV_TQ = 256
CONV_HALO = 32
CONV_ROWS = 32
POOL_TQ = 256
POOL_HALO = 16
ATT_TQ = 512
ATT_BB = 4
SAMPLE_BB = 32
ROUTE_TT = 128
PEER_CI = 4
PEER_EC = PEER_CI * PEER_KEYS

assert N_TOK % TOK_TILE == 0 and N_TOK % ROUTE_TT == 0


def _params(sem, vmem_mb):
    return pltpu.CompilerParams(dimension_semantics=sem, vmem_limit_bytes=vmem_mb * 1024 * 1024)


def _rms(x, g):
    return x * lax.rsqrt(jnp.mean(x * x, axis=-1, keepdims=True) + NORM_EPS) * g


def _linear_kernel(*refs, norm, has_res, emit_h):
    it = iter(refs)
    x_ref = next(it)
    g_ref = next(it) if norm else None
    w_ref = next(it)
    res_ref = next(it) if has_res else None
    o_ref = next(it)
    h_out_ref = next(it) if emit_h else None
    h_scr = next(it) if norm else None

    if norm:
        @pl.when(pl.program_id(1) == 0)
        def _():
            h = _rms(x_ref[...], g_ref[...]).astype(BF16)
            h_scr[...] = h
            if emit_h:
                h_out_ref[...] = h
        h = h_scr[...]
    else:
        h = x_ref[...].astype(BF16)
    y = jnp.dot(h, w_ref[...], preferred_element_type=F32)
    if has_res:
        y = res_ref[...] + y
    o_ref[...] = y


def _linear(x, w, g=None, res=None, emit_h=False, rows=None, row_off=0, tm=TOK_TILE, tn=LIN_TN):
    k, dout = w.shape
    rows = x.shape[0] if rows is None else rows
    assert rows % tm == 0 and dout % tn == 0 and row_off % tm == 0
    ro = row_off // tm
    norm = g is not None
    in_specs = [pl.BlockSpec((tm, k), lambda i, j: (i + ro, 0))]
    args = [x]
    if norm:
        in_specs.append(pl.BlockSpec((1, k), lambda i, j: (0, 0)))
        args.append(g)
    in_specs.append(pl.BlockSpec((k, tn), lambda i, j: (0, j)))
    args.append(w)
    if res is not None:
        in_specs.append(pl.BlockSpec((tm, tn), lambda i, j: (i + ro, j)))
        args.append(res)
    out_shape = [jax.ShapeDtypeStruct((rows, dout), F32)]
    out_specs = [pl.BlockSpec((tm, tn), lambda i, j: (i, j))]
    if emit_h:
        out_shape.append(jax.ShapeDtypeStruct((rows, k), BF16))
        out_specs.append(pl.BlockSpec((tm, k), lambda i, j: (i, 0)))
    outs = pl.pallas_call(
        functools.partial(_linear_kernel, norm=norm, has_res=res is not None, emit_h=emit_h),
        grid=(rows // tm, dout // tn),
        in_specs=in_specs,
        out_specs=out_specs,
        out_shape=out_shape,
        scratch_shapes=[pltpu.VMEM((tm, k), BF16)] if norm else [],
        compiler_params=_params(("parallel", "arbitrary"), 48),
        name="linear",
    )(*args)
    return outs if emit_h else outs[0]


def _rmsnorm_kernel(x_ref, g_ref, o_ref):
    o_ref[...] = _rms(x_ref[...], g_ref[...])


def _rmsnorm(x, g, rows, row_off, tm):
    ro = row_off // tm
    return pl.pallas_call(
        _rmsnorm_kernel,
        grid=(rows // tm,),
        in_specs=[pl.BlockSpec((tm, D_MODEL), lambda i: (i + ro, 0)),
                  pl.BlockSpec((1, D_MODEL), lambda i: (0, 0))],
        out_specs=pl.BlockSpec((tm, D_MODEL), lambda i: (i, 0)),
        out_shape=jax.ShapeDtypeStruct((rows, D_MODEL), F32),
        compiler_params=_params(("parallel",), 32),
        name="final_norm",
    )(x, g)


def _layernorm_silu(a, g, b):
    mu = jnp.mean(a, axis=-1, keepdims=True)
    ac = a - mu
    var = jnp.mean(ac * ac, axis=-1, keepdims=True)
    y = ac * lax.rsqrt(var + LN_EPS) * g + b
    return y * jax.nn.sigmoid(y)


def _conv_prompt_kernel(av_ref, ag_ref, bg_ref, cg_ref, bx_ref,
                        hav_ref, hag_ref, hcg_ref, hbx_ref,
                        wa_ref, ba_ref, lg_ref, lb_ref, wb_ref,
                        ab_ref, atail_ref, btail_ref,
                        aext, bext, aconv):
    i = pl.program_id(1)
    tq, halo = CONV_TQ, CONV_HALO
    keep = (i > 0).astype(F32)
    a_in = av_ref[...] * jax.nn.sigmoid(ag_ref[...])
    b_in = cg_ref[...] * bx_ref[...]
    aext[0:halo, :] = hav_ref[...] * jax.nn.sigmoid(hag_ref[...]) * keep
    aext[halo:halo + tq, :] = a_in
    bext[0:halo, :] = hcg_ref[...] * hbx_ref[...] * keep
    bext[halo:halo + tq, :] = b_in

    off_a = halo - (CONV_A_WIDTH - 1)
    for r0 in range(0, tq, CONV_ROWS):
        acc = jnp.broadcast_to(ba_ref[...], (CONV_ROWS, D_A))
        for k in range(CONV_A_WIDTH):
            acc = acc + wa_ref[k:k + 1, :] * aext[r0 + off_a + k:r0 + off_a + k + CONV_ROWS, :]
        aconv[r0:r0 + CONV_ROWS, :] = acc
    a = _layernorm_silu(aconv[...], lg_ref[...], lb_ref[...])

    off_b = halo - (CONV_B_WIDTH - 1)
    cb = wb_ref[0:1, :] * bext[off_b:off_b + tq, :]
    for k in range(1, CONV_B_WIDTH):
        cb = cb + wb_ref[k:k + 1, :] * bext[off_b + k:off_b + k + tq, :]
    b = bg_ref[...] * cb

    ab_ref[:, 0:D_A] = a.astype(BF16)
    ab_ref[:, D_A:D_A + D_B] = b.astype(BF16)

    @pl.when(i == pl.num_programs(1) - 1)
    def _():
        atail_ref[...] = a_in[tq - CONV_HALO:, :]
        btail_ref[...] = b_in[tq - V7X_SUBLANES:, :]


def _conv_prompt(p, wa, ba, lg, lb, wb):
    tq, halo = CONV_TQ, CONV_HALO
    nt = SEQ // tq
    cur = lambda c: pl.BlockSpec((tq, D_A), lambda b, i, c=c: (b * nt + i, c))
    hal = lambda c: pl.BlockSpec(
        (halo, D_A), lambda b, i, c=c: (jnp.maximum((b * SEQ + i * tq) // halo - 1, 0), c))
    full = lambda r: pl.BlockSpec((r, D_A), lambda b, i: (0, 0))
    return pl.pallas_call(
        _conv_prompt_kernel,
        grid=(BATCH, nt),
        in_specs=[cur(0), cur(1), cur(2), cur(3), cur(4), hal(0), hal(1), hal(3), hal(4),
                  full(CONV_A_WIDTH), full(1), full(1), full(1), full(CONV_B_WIDTH)],
        out_specs=[pl.BlockSpec((tq, D_MODEL), lambda b, i: (b * nt + i, 0)),
                   pl.BlockSpec((None, CONV_HALO, D_A), lambda b, i: (b, 0, 0)),
                   pl.BlockSpec((None, V7X_SUBLANES, D_B), lambda b, i: (b, 0, 0))],
        out_shape=[jax.ShapeDtypeStruct((N_PROMPT, D_MODEL), BF16),
                   jax.ShapeDtypeStruct((BATCH, CONV_HALO, D_A), F32),
                   jax.ShapeDtypeStruct((BATCH, V7X_SUBLANES, D_B), F32)],
        scratch_shapes=[pltpu.VMEM((halo + tq, D_A), F32), pltpu.VMEM((halo + tq, D_B), F32),
                        pltpu.VMEM((tq, D_A), F32)],
        compiler_params=_params(("parallel", "arbitrary"), 48),
        name="conv_prompt",
    )(p, p, p, p, p, p, p, p, p, wa, ba, lg, lb, wb)


def _conv_sample_kernel(av_ref, ag_ref, bg_ref, cg_ref, bx_ref, sa_ref, sb_ref,
                        wa_ref, ba_ref, lg_ref, lb_ref, wb_ref,
                        ab_ref, ain_ref, bin_ref):
    na, nb = CONV_A_WIDTH - 1, CONV_B_WIDTH - 1
    a_in = av_ref[...] * jax.nn.sigmoid(ag_ref[...])
    b_in = cg_ref[...] * bx_ref[...]
    conv_a = jnp.sum(sa_ref[...] * wa_ref[0:na, :][None], axis=1) + wa_ref[na:na + 1, :] * a_in
    a = _layernorm_silu(conv_a + ba_ref[...], lg_ref[...], lb_ref[...])
    conv_b = jnp.sum(sb_ref[...] * wb_ref[0:nb, :][None], axis=1) + wb_ref[nb:nb + 1, :] * b_in
    b = bg_ref[...] * conv_b
    ab_ref[:, 0:D_A] = a.astype(BF16)
    ab_ref[:, D_A:D_A + D_B] = b.astype(BF16)
    ain_ref[...] = a_in
    bin_ref[...] = b_in


def _conv_sample(p, state_a, state_b, wa, ba, lg, lb, wb):
    bb = SAMPLE_BB
    r0 = N_PROMPT // bb
    cur = lambda c: pl.BlockSpec((bb, D_A), lambda i, c=c: (r0 + i, c))
    full = lambda r: pl.BlockSpec((r, D_A), lambda i: (0, 0))
    return pl.pallas_call(
        _conv_sample_kernel,
        grid=(DEC_BATCH // bb,),
        in_specs=[cur(0), cur(1), cur(2), cur(3), cur(4),
                  pl.BlockSpec((bb, CONV_A_WIDTH - 1, D_A), lambda i: (i, 0, 0)),
                  pl.BlockSpec((bb, CONV_B_WIDTH - 1, D_B), lambda i: (i, 0, 0)),
                  full(CONV_A_WIDTH), full(1), full(1), full(1), full(CONV_B_WIDTH)],
        out_specs=[pl.BlockSpec((bb, D_MODEL), lambda i: (i, 0)),
                   pl.BlockSpec((bb, D_A), lambda i: (i, 0)),
                   pl.BlockSpec((bb, D_B), lambda i: (i, 0))],
        out_shape=[jax.ShapeDtypeStruct((DEC_BATCH, D_MODEL), BF16),
                   jax.ShapeDtypeStruct((DEC_BATCH, D_A), F32),
                   jax.ShapeDtypeStruct((DEC_BATCH, D_B), F32)],
        compiler_params=_params(("parallel",), 32),
        name="conv_sample",
    )(p, p, p, p, p, state_a, state_b, wa, ba, lg, lb, wb)


def _pool_prompt_kernel(x_ref, hx_ref, g_ref, d_ref, tail_ref, ext):
    i = pl.program_id(1)
    tq, halo = POOL_TQ, POOL_HALO
    keep = (i > 0).astype(F32)
    h = _rms(x_ref[...], g_ref[...])
    ext[0:halo, :] = _rms(hx_ref[...], g_ref[...]) * keep
    ext[halo:halo + tq, :] = h
    pos = (i * tq + lax.broadcasted_iota(jnp.int32, (tq, 1), 0)).astype(F32)
    for gi, w in enumerate(POOL_WINDOWS):
        c0 = gi * POOL_GROUP
        s = h[:, c0:c0 + POOL_GROUP]
        for k in range(1, w):
            s = s + ext[halo - k:halo - k + tq, c0:c0 + POOL_GROUP]
        cnt = jnp.minimum(float(w), pos + 1.0)
        d_ref[:, c0:c0 + POOL_GROUP] = (s / cnt - h[:, c0:c0 + POOL_GROUP]).astype(BF16)

    @pl.when(i == pl.num_programs(1) - 1)
    def _():
        tail_ref[...] = h[tq - POOL_HALO:, :]


def _pool_prompt(x, g):
    tq, halo = POOL_TQ, POOL_HALO
    nt = SEQ // tq
    return pl.pallas_call(
        _pool_prompt_kernel,
        grid=(BATCH, nt),
        in_specs=[pl.BlockSpec((tq, D_MODEL), lambda b, i: (b * nt + i, 0)),
                  pl.BlockSpec((halo, D_MODEL),
                               lambda b, i: (jnp.maximum((b * SEQ + i * tq) // halo - 1, 0), 0)),
                  pl.BlockSpec((1, D_MODEL), lambda b, i: (0, 0))],
        out_specs=[pl.BlockSpec((tq, D_MODEL), lambda b, i: (b * nt + i, 0)),
                   pl.BlockSpec((None, POOL_HALO, D_MODEL), lambda b, i: (b, 0, 0))],
        out_shape=[jax.ShapeDtypeStruct((N_PROMPT, D_MODEL), BF16),
                   jax.ShapeDtypeStruct((BATCH, POOL_HALO, D_MODEL), F32)],
        scratch_shapes=[pltpu.VMEM((halo + tq, D_MODEL), F32)],
        compiler_params=_params(("parallel", "arbitrary"), 48),
        name="pool_prompt",
    )(x, x, g)


def _pool_sample_kernel(x_ref, st_ref, g_ref, d_ref, h_ref):
    h = _rms(x_ref[...], g_ref[...])
    h_ref[...] = h
    for gi, w in enumerate(POOL_WINDOWS):
        c0 = gi * POOL_GROUP
        s = h[:, c0:c0 + POOL_GROUP] + jnp.sum(
            st_ref[:, POOL_CTX - (w - 1):POOL_CTX, c0:c0 + POOL_GROUP], axis=1)
        d_ref[:, c0:c0 + POOL_GROUP] = (s / float(w) - h[:, c0:c0 + POOL_GROUP]).astype(BF16)


def _pool_sample(x, state, g):
    bb = SAMPLE_BB
    r0 = N_PROMPT // bb
    return pl.pallas_call(
        _pool_sample_kernel,
        grid=(DEC_BATCH // bb,),
        in_specs=[pl.BlockSpec((bb, D_MODEL), lambda i: (r0 + i, 0)),
                  pl.BlockSpec((bb, POOL_CTX, D_MODEL), lambda i: (i, 0, 0)),
                  pl.BlockSpec((1, D_MODEL), lambda i: (0, 0))],
        out_specs=[pl.BlockSpec((bb, D_MODEL), lambda i: (i, 0)),
                   pl.BlockSpec((bb, D_MODEL), lambda i: (i, 0))],
        out_shape=[jax.ShapeDtypeStruct((DEC_BATCH, D_MODEL), BF16),
                   jax.ShapeDtypeStruct((DEC_BATCH, D_MODEL), F32)],
        compiler_params=_params(("parallel",), 32),
        name="pool_sample",
    )(x, state, g)


def _pool_proj_kernel(d_ref, w_ref, sc_ref, res_ref, o_ref):
    y = jnp.dot(d_ref[...], w_ref[...], preferred_element_type=F32)
    o_ref[...] = res_ref[...] + y * sc_ref[...]


def _pool_proj(d, w, scale, res, tm=TOK_TILE):
    pg = POOL_GROUP
    return pl.pallas_call(
        _pool_proj_kernel,
        grid=(N_TOK // tm, len(POOL_WINDOWS)),
        in_specs=[pl.BlockSpec((tm, pg), lambda i, j: (i, j)),
                  pl.BlockSpec((None, pg, pg), lambda i, j: (j, 0, 0)),
                  pl.BlockSpec((1, pg), lambda i, j: (0, j)),
                  pl.BlockSpec((tm, pg), lambda i, j: (i, j))],
        out_specs=pl.BlockSpec((tm, pg), lambda i, j: (i, j)),
        out_shape=jax.ShapeDtypeStruct((N_TOK, D_MODEL), F32),
        compiler_params=_params(("parallel", "arbitrary"), 32),
        name="pool_proj",
    )(d, w, scale, res)


def _attn_prompt_kernel(q_ref, k_ref, v_ref, o_ref):
    hd = MEM_HEAD_DIM
    for h in range(MEM_HEADS):
        q = q_ref[:, h * hd:(h + 1) * hd].astype(BF16)
        k = k_ref[:, h * hd:(h + 1) * hd].astype(BF16)
        v = v_ref[:, h * hd:(h + 1) * hd].astype(BF16)
        s = lax.dot_general(q, k, (((1,), (1,)), ((), ())), preferred_element_type=F32) * hd ** -0.5
        e = jnp.exp(s - jnp.max(s, axis=-1, keepdims=True))
        p = (e / jnp.sum(e, axis=-1, keepdims=True)).astype(BF16)
        o_ref[:, h * hd:(h + 1) * hd] = jnp.dot(p, v, preferred_element_type=F32).astype(BF16)


def _attn_prompt(q, k, v):
    tq = ATT_TQ
    nt = SEQ // tq
    return pl.pallas_call(
        _attn_prompt_kernel,
        grid=(BATCH, nt),
        in_specs=[pl.BlockSpec((tq, D_MODEL), lambda b, i: (b * nt + i, 0)),
                  pl.BlockSpec((None, N_MEM, D_MODEL), lambda b, i: (b, 0, 0)),
                  pl.BlockSpec((None, N_MEM, D_MODEL), lambda b, i: (b, 0, 0))],
        out_specs=pl.BlockSpec((tq, D_MODEL), lambda b, i: (b * nt + i, 0)),
        out_shape=jax.ShapeDtypeStruct((N_PROMPT, D_MODEL), BF16),
        compiler_params=_params(("parallel", "arbitrary"), 32),
        name="attn_prompt",
    )(q, k, v)


def _attn_sample_kernel(q_ref, k_ref, v_ref, o_ref):
    for b in range(ATT_BB):
        prod = k_ref[b] * q_ref[b][None]
        s = jnp.sum(prod, axis=-1, keepdims=True) * MEM_HEAD_DIM ** -0.5
        e = jnp.exp(s - jnp.max(s, axis=0, keepdims=True))
        p = e / jnp.sum(e, axis=0, keepdims=True)
        o_ref[b] = jnp.sum(p * v_ref[b], axis=0)


def _attn_sample(q, k, v, layer):
    bb = ATT_BB
    cache = pl.BlockSpec((None, bb, N_MEM, MEM_HEADS, MEM_HEAD_DIM), lambda i: (layer, i, 0, 0, 0))
    qo = pl.BlockSpec((bb, MEM_HEADS, MEM_HEAD_DIM), lambda i: (i, 0, 0))
    return pl.pallas_call(
        _attn_sample_kernel,
        grid=(DEC_BATCH // bb,),
        in_specs=[qo, cache, cache],
        out_specs=qo,
        out_shape=jax.ShapeDtypeStruct((DEC_BATCH, MEM_HEADS, MEM_HEAD_DIM), F32),
        compiler_params=_params(("parallel",), 48),
        name="attn_sample",
    )(q, k, v)


def _top16(s, ids, tiebreak):
    work = s
    rank = jnp.full(s.shape, float(PEER_TOPK), F32)
    vals = []
    for a in range(PEER_TOPK):
        m = jnp.max(work, axis=0, keepdims=True)
        sel = work == m
        if tiebreak:
            first = jnp.min(jnp.where(sel, ids, jnp.inf), axis=0, keepdims=True)
            sel = ids == first
        rank = jnp.where(sel, float(a), rank)
        work = jnp.where(sel, -jnp.inf, work)
        vals.append(m)
    return rank, vals


def _count_selected(rank):
    return jnp.sum((rank < float(PEER_TOPK)).astype(F32), axis=0, keepdims=True)


_CAND_ROWS = tuple(PEER_TOPK // (a + 1) for a in range(V7X_SUBLANES))


def _route_compute(q_ref, keys_ref, p_ref, t0_ref, qq_ref, r1_ref, tiebreak):
    tt = ROUTE_TT
    sub = V7X_SUBLANES
    iota_k = lax.broadcasted_iota(jnp.int32, (PEER_KEYS, tt), 0).astype(F32)
    iota_8 = lax.broadcasted_iota(jnp.int32, (sub, tt), 0).astype(F32)
    iota_16 = lax.broadcasted_iota(jnp.int32, (PEER_TOPK, tt), 0).astype(F32)
    cand_ids = jnp.concatenate(
        [iota_16] + [a * float(PEER_TOPK) + iota_8 for a in range(1, sub)]
        + [(iota_8 + float(sub)) * float(PEER_TOPK)], axis=0)
    bad = jnp.zeros((1, tt), F32)
    for h in range(PEER_HEADS):
        scores, ranks, svals = [], [], []
        for part in range(2):
            c0 = (2 * h + part) * PEER_HALF
            qh = q_ref[:, c0:c0 + PEER_HALF].astype(BF16)
            s = lax.dot_general(keys_ref[h, part], qh, (((1,), (1,)), ((), ())),
                                preferred_element_type=F32)
            rank, vals = _top16(s, iota_k, tiebreak)
            bad = bad + jnp.abs(_count_selected(rank) - float(PEER_TOPK))
            scores.append(s)
            ranks.append(rank)
            svals.append(vals)
        sv0 = jnp.concatenate(svals[0], axis=0)
        sv1 = jnp.concatenate(svals[1], axis=0)
        groups = [svals[0][0] + sv1]
        for a in range(1, sub):
            g = svals[0][a] + sv1[0:sub, :]
            groups.append(jnp.where(iota_8 < float(_CAND_ROWS[a]), g, -jnp.inf))
        groups.append(sv0[sub:, :] + svals[1][0])
        cand = jnp.concatenate(groups, axis=0)
        crank, _ = _top16(cand, cand_ids, tiebreak)
        selm = crank < float(PEER_TOPK)
        bad = bad + jnp.abs(_count_selected(crank) - float(PEER_TOPK))
        z = jnp.sum(jnp.where(selm, jnp.exp(cand - cand[0:1, :]), 0.0), axis=0, keepdims=True)
        e0 = jnp.where(ranks[0] < float(PEER_TOPK), jnp.exp(scores[0] - svals[0][0]), 0.0)
        e1 = jnp.where(ranks[1] < float(PEER_TOPK), jnp.exp(scores[1] - svals[1][0]), 0.0)
        kept = selm.astype(F32)
        counts = [jnp.sum(kept[0:PEER_TOPK, :], axis=0, keepdims=True)]
        for a in range(1, sub):
            r0 = PEER_TOPK + (a - 1) * sub
            counts.append(jnp.sum(kept[r0:r0 + sub, :], axis=0, keepdims=True))
        r0 = PEER_TOPK + (sub - 1) * sub
        counts += [kept[r0 + k:r0 + k + 1, :] for k in range(sub)]
        t0 = jnp.zeros((PEER_KEYS, tt), F32)
        for a in range(PEER_TOPK):
            t0 = jnp.where(ranks[0] == float(a), counts[a], t0)
        p_ref[h] = e0 / z
        t0_ref[h] = t0
        qq_ref[h] = e1.astype(BF16)
        r1_ref[h] = ranks[1].astype(BF16)
    return bad


def _route_kernel(q_ref, keys_ref, p_ref, t0_ref, qq_ref, r1_ref):
    outs = (p_ref, t0_ref, qq_ref, r1_ref)
    bad = _route_compute(q_ref, keys_ref, *outs, tiebreak=False)

    @pl.when(jnp.max(bad) > 0.0)
    def _():
        _route_compute(q_ref, keys_ref, *outs, tiebreak=True)


def _route(q, keys):
    tt = ROUTE_TT
    blk = pl.BlockSpec((PEER_HEADS, PEER_KEYS, tt), lambda i: (0, 0, i))
    shp = lambda dt: jax.ShapeDtypeStruct((PEER_HEADS, PEER_KEYS, N_TOK), dt)
    return pl.pallas_call(
        _route_kernel,
        grid=(N_TOK // tt,),
        in_specs=[pl.BlockSpec((tt, D_MODEL), lambda i: (i, 0)),
                  pl.BlockSpec((PEER_HEADS, 2, PEER_KEYS, PEER_HALF), lambda i: (0, 0, 0, 0))],
        out_specs=[blk, blk, blk, blk],
        out_shape=[shp(F32), shp(F32), shp(BF16), shp(BF16)],
        compiler_params=_params(("parallel",), 32),
        name="peer_route",
    )(q, keys)


_SCORE_COLS = ((0, 256), (256, 256), (512, TOK_TILE - 512))
_OUT_COLS = 512


def _peer_score_piece(u_ref, r0, z_ref, s_ref, c0, w):
    s_ref[:, c0:c0 + w] = lax.dot_general(
        u_ref[r0:r0 + PEER_EC, :], z_ref[c0:c0 + w, :], (((1,), (1,)), ((), ())),
        preferred_element_type=F32)


def _peer_act_piece(s_ref, p_ref, t0_ref, row, qq_ref, r1_ref, a_ref, il, valid):
    zero = jnp.zeros((), BF16)
    g = None
    for h in range(PEER_HEADS):
        prow = p_ref[h, row:row + 1, :].astype(BF16)
        trow = t0_ref[h, row:row + 1, :].astype(BF16)
        gh = jnp.where(r1_ref[h] < trow, qq_ref[h] * prow, zero)
        g = gh if g is None else g + gh
    s = s_ref[il * PEER_KEYS:(il + 1) * PEER_KEYS, :]
    act = jax.nn.gelu(s, approximate=True).astype(BF16) * g
    if valid is not None:
        act = jnp.where(valid, act, zero)
    a_ref[:, il * PEER_KEYS:(il + 1) * PEER_KEYS] = act.T


def _peer_out_piece(a_ref, v_ref, r0, o_ref, n):
    c0 = n * _OUT_COLS
    o_ref[:, c0:c0 + _OUT_COLS] += jnp.dot(a_ref[...], v_ref[r0:r0 + PEER_EC, c0:c0 + _OUT_COLS],
                                           preferred_element_type=F32)


def _interleave(*streams):
    for k in range(max(len(s) for s in streams)):
        for s in streams:
            if k < len(s):
                s[k]()


def _peer_dense_kernel(z_ref, u_ref, v_ref, plo_ref, tlo_ref, phi_ref, thi_ref, qq_ref, r1_ref, res_ref,
                       o_ref, s0, s1, a0, a1):
    step = pl.program_id(1)
    ci = PEER_CI
    n_out = D_MODEL // _OUT_COLS

    @pl.when(step == 0)
    def _():
        o_ref[...] = res_ref[...]
        a0[...] = jnp.zeros(a0.shape, BF16)
        s1[...] = jnp.zeros(s1.shape, F32)

    part = functools.partial
    _interleave(
        [part(_peer_score_piece, u_ref, 0, z_ref, s0, c0, w) for c0, w in _SCORE_COLS],
        [part(_peer_act_piece, s1, plo_ref, tlo_ref, ci + il, qq_ref, r1_ref, a1, il, step > 0)
         for il in range(ci)],
        [part(_peer_out_piece, a0, v_ref, 0, o_ref, n) for n in range(n_out)])
    _interleave(
        [part(_peer_score_piece, u_ref, PEER_EC, z_ref, s1, c0, w) for c0, w in _SCORE_COLS],
        [part(_peer_act_piece, s0, phi_ref, thi_ref, il, qq_ref, r1_ref, a0, il, None)
         for il in range(ci)],
        [part(_peer_out_piece, a1, v_ref, PEER_EC, o_ref, n) for n in range(n_out)])


def _peer_dense(z, u, v, p, t0, qq, r1, res, layer):
    tt, ec = TOK_TILE, PEER_EC
    n_pairs = u.shape[1] // (2 * ec)
    once = pl.Buffered(1)
    lo = pl.BlockSpec((PEER_HEADS, 2 * PEER_CI, tt), lambda t, s: (0, jnp.maximum(s - 1, 0), t))
    hi = pl.BlockSpec((PEER_HEADS, 2 * PEER_CI, tt), lambda t, s: (0, jnp.minimum(s, n_pairs - 1), t))
    second = pl.BlockSpec((PEER_HEADS, PEER_KEYS, tt), lambda t, s: (0, 0, t))
    return pl.pallas_call(
        _peer_dense_kernel,
        grid=(N_TOK // tt, n_pairs + 1),
        in_specs=[pl.BlockSpec((tt, D_MODEL), lambda t, s: (t, 0)),
                  pl.BlockSpec((None, 2 * ec, D_MODEL), lambda t, s: (layer, jnp.minimum(s, n_pairs - 1), 0)),
                  pl.BlockSpec((None, 2 * ec, D_MODEL), lambda t, s: (layer, jnp.maximum(s - 1, 0), 0)),
                  lo, lo, hi, hi, second, second,
                  pl.BlockSpec((tt, D_MODEL), lambda t, s: (t, 0), pipeline_mode=once)],
        out_specs=pl.BlockSpec((tt, D_MODEL), lambda t, s: (t, 0)),
        out_shape=jax.ShapeDtypeStruct((N_TOK, D_MODEL), F32),
        scratch_shapes=[pltpu.VMEM((ec, tt), F32), pltpu.VMEM((ec, tt), F32),
                        pltpu.VMEM((tt, ec), BF16), pltpu.VMEM((tt, ec), BF16)],
        compiler_params=_params(("parallel", "arbitrary"), 56),
        name="peer_dense",
    )(z, u, v, p, t0, p, t0, qq, r1, res)


def _row(v):
    return v.reshape(1, -1)


def kernel(x_prompt, x_sample, mem_prompt, cache_mem_k, cache_mem_v, state_conv_a, state_conv_b, state_pool, norm_mix_g, norm_mem_g, norm_ffn_g, mem_norm_g, final_norm_g, w_in, conv_a_w, conv_a_b, ln_a_g, ln_a_b, conv_b_w, w_out, pool_w, pool_scale, mem_wq, mem_wk, mem_wv, mem_wo, peer_wq, peer_keys, peer_u, peer_v):
    bf = lambda a: a.astype(BF16)
    x = jnp.concatenate([x_prompt.reshape(N_PROMPT, D_MODEL), x_sample.reshape(DEC_BATCH, D_MODEL)], axis=0)

    mem = mem_prompt.reshape(BATCH * N_MEM, D_MODEL)
    pk, pv = [], []
    for l in range(DEPTH):
        wkv = jnp.concatenate([bf(mem_wk[l]), bf(mem_wv[l])], axis=1)
        kv = _linear(mem, wkv, g=_row(mem_norm_g[l]), tm=BATCH * N_MEM // 2)
        pk.append(kv[:, :D_MODEL].reshape(BATCH, N_MEM, D_MODEL))
        pv.append(kv[:, D_MODEL:].reshape(BATCH, N_MEM, D_MODEL))

    u_tab, v_tab = bf(peer_u), bf(peer_v)
    new_a, new_b, new_pool = [], [], []
    for l in range(DEPTH):
        if l % 2 == 0:
            e = l // 2
            p = _linear(x, bf(w_in[e]), g=_row(norm_mix_g[l]))
            cw = (conv_a_w[e], _row(conv_a_b[e]), _row(ln_a_g[e]), _row(ln_a_b[e]), conv_b_w[e])
            ab_p, a_tail, b_tail = _conv_prompt(p, *cw)
            ab_s, a_in_s, b_in_s = _conv_sample(p, state_conv_a[e], state_conv_b[e], *cw)
            x = _linear(jnp.concatenate([ab_p, ab_s], axis=0), bf(w_out[e]), res=x)
            new_a.append((a_tail[:, CONV_HALO - (CONV_A_WIDTH - 1):],
                          jnp.concatenate([state_conv_a[e][:, 1:], a_in_s[:, None]], axis=1)))
            new_b.append((b_tail[:, V7X_SUBLANES - (CONV_B_WIDTH - 1):],
                          jnp.concatenate([state_conv_b[e][:, 1:], b_in_s[:, None]], axis=1)))
        else:
            o = l // 2
            g = _row(norm_mix_g[l])
            d_p, h_tail = _pool_prompt(x, g)
            d_s, h_s = _pool_sample(x, state_pool[o], g)
            x = _pool_proj(jnp.concatenate([d_p, d_s], axis=0), bf(pool_w[o]), _row(pool_scale[o]), x)
            new_pool.append((h_tail[:, POOL_HALO - POOL_CTX:],
                             jnp.concatenate([state_pool[o][:, 1:], h_s[:, None]], axis=1)))

        q = _linear(x, bf(mem_wq[l]), g=_row(norm_mem_g[l]))
        o_p = _attn_prompt(q, pk[l], pv[l])
        o_s = _attn_sample(q[N_PROMPT:].reshape(DEC_BATCH, MEM_HEADS, MEM_HEAD_DIM), cache_mem_k, cache_mem_v, l)
        x = _linear(jnp.concatenate([o_p, bf(o_s.reshape(DEC_BATCH, D_MODEL))], axis=0), bf(mem_wo[l]), res=x)

        pq, z = _linear(x, bf(peer_wq[l]), g=_row(norm_ffn_g[l]), emit_h=True)
        gp, gt0, gq, gr1 = _route(pq, bf(peer_keys[l]))
        x = _peer_dense(z, u_tab, v_tab, gp, gt0, gq, gr1, x, l)

    y_prompt = _rmsnorm(x, _row(final_norm_g), N_PROMPT, 0, 512).reshape(BATCH, SEQ, D_MODEL)
    y_sample = _rmsnorm(x, _row(final_norm_g), DEC_BATCH, N_PROMPT, DEC_BATCH).reshape(DEC_BATCH, 1, D_MODEL)
    shape_kv = (DEPTH, BATCH, N_MEM, MEM_HEADS, MEM_HEAD_DIM)
    return (y_prompt, y_sample,
            jnp.stack(pk).reshape(shape_kv), jnp.stack(pv).reshape(shape_kv),
            jnp.stack([a for a, _ in new_a]), jnp.stack([b for b, _ in new_b]),
            jnp.stack([c for c, _ in new_pool]),
            jnp.stack([a for _, a in new_a]), jnp.stack([b for _, b in new_b]),
            jnp.stack([c for _, c in new_pool]))
```

```python
import functools

import jax
import jax.numpy as jnp
from jax import lax
from jax.experimental import pallas as pl
from jax.experimental.pallas import tpu as pltpu

F32 = jnp.float32
BF16 = jnp.bfloat16

D_MODEL = 2048
BATCH = 4
SEQ = 2048
DEPTH = 2
DEC_BATCH = 128
N_PROMPT = BATCH * SEQ
N_TOK = N_PROMPT + DEC_BATCH
D_A = D_MODEL // 2
D_B = D_MODEL // 2
CONV_A_WIDTH = 31
CONV_B_WIDTH = 3
IN_COLS = 2 * D_A + 3 * D_B
POOL_WINDOWS = (2, 4, 8, 16)
POOL_GROUP = D_MODEL // len(POOL_WINDOWS)
POOL_CTX = max(POOL_WINDOWS) - 1
N_MEM = 256
MEM_HEADS = 4
MEM_HEAD_DIM = D_MODEL // MEM_HEADS
PEER_HEADS = 8
PEER_KEYS = 128
PEER_TOPK = 16
PEER_HALF = 128
NORM_EPS = 1e-6
LN_EPS = 1e-5

V7X_LANES = 128
V7X_SUBLANES = 8
V7X_VMEM_BYTES = 64 * 1024 * 1024

TOK_TILE = 640
LIN_TN = 1024
CONV_TQ = 256
CONV_HALO = 32
CONV_ROWS = 32
POOL_TQ = 256
POOL_HALO = 16
ATT_TQ = 512
ATT_BB = 4
SAMPLE_BB = 32
ROUTE_TT = 128
PEER_CI = 4
PEER_EC = PEER_CI * PEER_KEYS

assert N_TOK % TOK_TILE == 0 and N_TOK % ROUTE_TT == 0


def _params(sem, vmem_mb):
    return pltpu.CompilerParams(dimension_semantics=sem, vmem_limit_bytes=vmem_mb * 1024 * 1024)


def _rms(x, g):
    return x * lax.rsqrt(jnp.mean(x * x, axis=-1, keepdims=True) + NORM_EPS) * g


def _linear_kernel(*refs, norm, has_res, emit_h):
    it = iter(refs)
    x_ref = next(it)
    g_ref = next(it) if norm else None
    w_ref = next(it)
    res_ref = next(it) if has_res else None
    o_ref = next(it)
    h_out_ref = next(it) if emit_h else None
    h_scr = next(it) if norm else None

    if norm:
        @pl.when(pl.program_id(1) == 0)
        def _():
            h = _rms(x_ref[...], g_ref[...]).astype(BF16)
            h_scr[...] = h
            if emit_h:
                h_out_ref[...] = h
        h = h_scr[...]
    else:
        h = x_ref[...].astype(BF16)
    y = jnp.dot(h, w_ref[...], preferred_element_type=F32)
    if has_res:
        y = res_ref[...] + y
    o_ref[...] = y


def _linear(x, w, g=None, res=None, emit_h=False, rows=None, row_off=0, tm=TOK_TILE, tn=LIN_TN):
    k, dout = w.shape
    rows = x.shape[0] if rows is None else rows
    assert rows % tm == 0 and dout % tn == 0 and row_off % tm == 0
    ro = row_off // tm
    norm = g is not None
    in_specs = [pl.BlockSpec((tm, k), lambda i, j: (i + ro, 0))]
    args = [x]
    if norm:
        in_specs.append(pl.BlockSpec((1, k), lambda i, j: (0, 0)))
        args.append(g)
    in_specs.append(pl.BlockSpec((k, tn), lambda i, j: (0, j)))
    args.append(w)
    if res is not None:
        in_specs.append(pl.BlockSpec((tm, tn), lambda i, j: (i + ro, j)))
        args.append(res)
    out_shape = [jax.ShapeDtypeStruct((rows, dout), F32)]
    out_specs = [pl.BlockSpec((tm, tn), lambda i, j: (i, j))]
    if emit_h:
        out_shape.append(jax.ShapeDtypeStruct((rows, k), BF16))
        out_specs.append(pl.BlockSpec((tm, k), lambda i, j: (i, 0)))
    outs = pl.pallas_call(
        functools.partial(_linear_kernel, norm=norm, has_res=res is not None, emit_h=emit_h),
        grid=(rows // tm, dout // tn),
        in_specs=in_specs,
        out_specs=out_specs,
        out_shape=out_shape,
        scratch_shapes=[pltpu.VMEM((tm, k), BF16)] if norm else [],
        compiler_params=_params(("parallel", "arbitrary"), 48),
        name="linear",
    )(*args)
    return outs if emit_h else outs[0]


def _rmsnorm_kernel(x_ref, g_ref, o_ref):
    o_ref[...] = _rms(x_ref[...], g_ref[...])


def _rmsnorm(x, g, rows, row_off, tm):
    ro = row_off // tm
    return pl.pallas_call(
        _rmsnorm_kernel,
        grid=(rows // tm,),
        in_specs=[pl.BlockSpec((tm, D_MODEL), lambda i: (i + ro, 0)),
                  pl.BlockSpec((1, D_MODEL), lambda i: (0, 0))],
        out_specs=pl.BlockSpec((tm, D_MODEL), lambda i: (i, 0)),
        out_shape=jax.ShapeDtypeStruct((rows, D_MODEL), F32),
        compiler_params=_params(("parallel",), 32),
        name="final_norm",
    )(x, g)


def _layernorm_silu(a, g, b):
    mu = jnp.mean(a, axis=-1, keepdims=True)
    ac = a - mu
    var = jnp.mean(ac * ac, axis=-1, keepdims=True)
    y = ac * lax.rsqrt(var + LN_EPS) * g + b
    return y * jax.nn.sigmoid(y)


def _conv_prompt_kernel(av_ref, ag_ref, bg_ref, cg_ref, bx_ref,
                        hav_ref, hag_ref, hcg_ref, hbx_ref,
                        wa_ref, ba_ref, lg_ref, lb_ref, wb_ref,
                        ab_ref, atail_ref, btail_ref,
                        aext, bext, aconv):
    i = pl.program_id(1)
    tq, halo = CONV_TQ, CONV_HALO
    keep = (i > 0).astype(F32)
    a_in = av_ref[...] * jax.nn.sigmoid(ag_ref[...])
    b_in = cg_ref[...] * bx_ref[...]
    aext[0:halo, :] = hav_ref[...] * jax.nn.sigmoid(hag_ref[...]) * keep
    aext[halo:halo + tq, :] = a_in
    bext[0:halo, :] = hcg_ref[...] * hbx_ref[...] * keep
    bext[halo:halo + tq, :] = b_in

    off_a = halo - (CONV_A_WIDTH - 1)
    for r0 in range(0, tq, CONV_ROWS):
        acc = jnp.broadcast_to(ba_ref[...], (CONV_ROWS, D_A))
        for k in range(CONV_A_WIDTH):
            acc = acc + wa_ref[k:k + 1, :] * aext[r0 + off_a + k:r0 + off_a + k + CONV_ROWS, :]
        aconv[r0:r0 + CONV_ROWS, :] = acc
    a = _layernorm_silu(aconv[...], lg_ref[...], lb_ref[...])

    off_b = halo - (CONV_B_WIDTH - 1)
    cb = wb_ref[0:1, :] * bext[off_b:off_b + tq, :]
    for k in range(1, CONV_B_WIDTH):
        cb = cb + wb_ref[k:k + 1, :] * bext[off_b + k:off_b + k + tq, :]
    b = bg_ref[...] * cb

    ab_ref[:, 0:D_A] = a.astype(BF16)
    ab_ref[:, D_A:D_A + D_B] = b.astype(BF16)

    @pl.when(i == pl.num_programs(1) - 1)
    def _():
        atail_ref[...] = a_in[tq - CONV_HALO:, :]
        btail_ref[...] = b_in[tq - V7X_SUBLANES:, :]


def _conv_prompt(p, wa, ba, lg, lb, wb):
    tq, halo = CONV_TQ, CONV_HALO
    nt = SEQ // tq
    cur = lambda c: pl.BlockSpec((tq, D_A), lambda b, i, c=c: (b * nt + i, c))
    hal = lambda c: pl.BlockSpec(
        (halo, D_A), lambda b, i, c=c: (jnp.maximum((b * SEQ + i * tq) // halo - 1, 0), c))
    full = lambda r: pl.BlockSpec((r, D_A), lambda b, i: (0, 0))
    return pl.pallas_call(
        _conv_prompt_kernel,
        grid=(BATCH, nt),
        in_specs=[cur(0), cur(1), cur(2), cur(3), cur(4), hal(0), hal(1), hal(3), hal(4),
                  full(CONV_A_WIDTH), full(1), full(1), full(1), full(CONV_B_WIDTH)],
        out_specs=[pl.BlockSpec((tq, D_MODEL), lambda b, i: (b * nt + i, 0)),
                   pl.BlockSpec((None, CONV_HALO, D_A), lambda b, i: (b, 0, 0)),
                   pl.BlockSpec((None, V7X_SUBLANES, D_B), lambda b, i: (b, 0, 0))],
        out_shape=[jax.ShapeDtypeStruct((N_PROMPT, D_MODEL), BF16),
                   jax.ShapeDtypeStruct((BATCH, CONV_HALO, D_A), F32),
                   jax.ShapeDtypeStruct((BATCH, V7X_SUBLANES, D_B), F32)],
        scratch_shapes=[pltpu.VMEM((halo + tq, D_A), F32), pltpu.VMEM((halo + tq, D_B), F32),
                        pltpu.VMEM((tq, D_A), F32)],
        compiler_params=_params(("parallel", "arbitrary"), 48),
        name="conv_prompt",
    )(p, p, p, p, p, p, p, p, p, wa, ba, lg, lb, wb)


def _conv_sample_kernel(av_ref, ag_ref, bg_ref, cg_ref, bx_ref, sa_ref, sb_ref,
                        wa_ref, ba_ref, lg_ref, lb_ref, wb_ref,
                        ab_ref, ain_ref, bin_ref):
    na, nb = CONV_A_WIDTH - 1, CONV_B_WIDTH - 1
    a_in = av_ref[...] * jax.nn.sigmoid(ag_ref[...])
    b_in = cg_ref[...] * bx_ref[...]
    conv_a = jnp.sum(sa_ref[...] * wa_ref[0:na, :][None], axis=1) + wa_ref[na:na + 1, :] * a_in
    a = _layernorm_silu(conv_a + ba_ref[...], lg_ref[...], lb_ref[...])
    conv_b = jnp.sum(sb_ref[...] * wb_ref[0:nb, :][None], axis=1) + wb_ref[nb:nb + 1, :] * b_in
    b = bg_ref[...] * conv_b
    ab_ref[:, 0:D_A] = a.astype(BF16)
    ab_ref[:, D_A:D_A + D_B] = b.astype(BF16)
    ain_ref[...] = a_in
    bin_ref[...] = b_in


def _conv_sample(p, state_a, state_b, wa, ba, lg, lb, wb):
    bb = SAMPLE_BB
    r0 = N_PROMPT // bb
    cur = lambda c: pl.BlockSpec((bb, D_A), lambda i, c=c: (r0 + i, c))
    full = lambda r: pl.BlockSpec((r, D_A), lambda i: (0, 0))
    return pl.pallas_call(
        _conv_sample_kernel,
        grid=(DEC_BATCH // bb,),
        in_specs=[cur(0), cur(1), cur(2), cur(3), cur(4),
                  pl.BlockSpec((bb, CONV_A_WIDTH - 1, D_A), lambda i: (i, 0, 0)),
                  pl.BlockSpec((bb, CONV_B_WIDTH - 1, D_B), lambda i: (i, 0, 0)),
                  full(CONV_A_WIDTH), full(1), full(1), full(1), full(CONV_B_WIDTH)],
        out_specs=[pl.BlockSpec((bb, D_MODEL), lambda i: (i, 0)),
                   pl.BlockSpec((bb, D_A), lambda i: (i, 0)),
                   pl.BlockSpec((bb, D_B), lambda i: (i, 0))],
        out_shape=[jax.ShapeDtypeStruct((DEC_BATCH, D_MODEL), BF16),
                   jax.ShapeDtypeStruct((DEC_BATCH, D_A), F32),
                   jax.ShapeDtypeStruct((DEC_BATCH, D_B), F32)],
        compiler_params=_params(("parallel",), 32),
        name="conv_sample",
    )(p, p, p, p, p, state_a, state_b, wa, ba, lg, lb, wb)


def _pool_prompt_kernel(x_ref, hx_ref, g_ref, d_ref, tail_ref, ext):
    i = pl.program_id(1)
    tq, halo = POOL_TQ, POOL_HALO
    keep = (i > 0).astype(F32)
    h = _rms(x_ref[...], g_ref[...])
    ext[0:halo, :] = _rms(hx_ref[...], g_ref[...]) * keep
    ext[halo:halo + tq, :] = h
    pos = (i * tq + lax.broadcasted_iota(jnp.int32, (tq, 1), 0)).astype(F32)
    for gi, w in enumerate(POOL_WINDOWS):
        c0 = gi * POOL_GROUP
        s = h[:, c0:c0 + POOL_GROUP]
        for k in range(1, w):
            s = s + ext[halo - k:halo - k + tq, c0:c0 + POOL_GROUP]
        cnt = jnp.minimum(float(w), pos + 1.0)
        d_ref[:, c0:c0 + POOL_GROUP] = (s / cnt - h[:, c0:c0 + POOL_GROUP]).astype(BF16)

    @pl.when(i == pl.num_programs(1) - 1)
    def _():
        tail_ref[...] = h[tq - POOL_HALO:, :]


def _pool_prompt(x, g):
    tq, halo = POOL_TQ, POOL_HALO
    nt = SEQ // tq
    return pl.pallas_call(
        _pool_prompt_kernel,
        grid=(BATCH, nt),
        in_specs=[pl.BlockSpec((tq, D_MODEL), lambda b, i: (b * nt + i, 0)),
                  pl.BlockSpec((halo, D_MODEL),
                               lambda b, i: (jnp.maximum((b * SEQ + i * tq) // halo - 1, 0), 0)),
                  pl.BlockSpec((1, D_MODEL), lambda b, i: (0, 0))],
        out_specs=[pl.BlockSpec((tq, D_MODEL), lambda b, i: (b * nt + i, 0)),
                   pl.BlockSpec((None, POOL_HALO, D_MODEL), lambda b, i: (b, 0, 0))],
        out_shape=[jax.ShapeDtypeStruct((N_PROMPT, D_MODEL), BF16),
                   jax.ShapeDtypeStruct((BATCH, POOL_HALO, D_MODEL), F32)],
        scratch_shapes=[pltpu.VMEM((halo + tq, D_MODEL), F32)],
        compiler_params=_params(("parallel", "arbitrary"), 48),
        name="pool_prompt",
    )(x, x, g)


def _pool_sample_kernel(x_ref, st_ref, g_ref, d_ref, h_ref):
    h = _rms(x_ref[...], g_ref[...])
    h_ref[...] = h
    for gi, w in enumerate(POOL_WINDOWS):
        c0 = gi * POOL_GROUP
        s = h[:, c0:c0 + POOL_GROUP] + jnp.sum(
            st_ref[:, POOL_CTX - (w - 1):POOL_CTX, c0:c0 + POOL_GROUP], axis=1)
        d_ref[:, c0:c0 + POOL_GROUP] = (s / float(w) - h[:, c0:c0 + POOL_GROUP]).astype(BF16)


def _pool_sample(x, state, g):
    bb = SAMPLE_BB
    r0 = N_PROMPT // bb
    return pl.pallas_call(
        _pool_sample_kernel,
        grid=(DEC_BATCH // bb,),
        in_specs=[pl.BlockSpec((bb, D_MODEL), lambda i: (r0 + i, 0)),
                  pl.BlockSpec((bb, POOL_CTX, D_MODEL), lambda i: (i, 0, 0)),
                  pl.BlockSpec((1, D_MODEL), lambda i: (0, 0))],
        out_specs=[pl.BlockSpec((bb, D_MODEL), lambda i: (i, 0)),
                   pl.BlockSpec((bb, D_MODEL), lambda i: (i, 0))],
        out_shape=[jax.ShapeDtypeStruct((DEC_BATCH, D_MODEL), BF16),
                   jax.ShapeDtypeStruct((DEC_BATCH, D_MODEL), F32)],
        compiler_params=_params(("parallel",), 32),
        name="pool_sample",
    )(x, state, g)


def _pool_proj_kernel(d_ref, w_ref, sc_ref, res_ref, o_ref):
    y = jnp.dot(d_ref[...], w_ref[...], preferred_element_type=F32)
    o_ref[...] = res_ref[...] + y * sc_ref[...]


def _pool_proj(d, w, scale, res, tm=TOK_TILE):
    pg = POOL_GROUP
    return pl.pallas_call(
        _pool_proj_kernel,
        grid=(N_TOK // tm, len(POOL_WINDOWS)),
        in_specs=[pl.BlockSpec((tm, pg), lambda i, j: (i, j)),
                  pl.BlockSpec((None, pg, pg), lambda i, j: (j, 0, 0)),
                  pl.BlockSpec((1, pg), lambda i, j: (0, j)),
                  pl.BlockSpec((tm, pg), lambda i, j: (i, j))],
        out_specs=pl.BlockSpec((tm, pg), lambda i, j: (i, j)),
        out_shape=jax.ShapeDtypeStruct((N_TOK, D_MODEL), F32),
        compiler_params=_params(("parallel", "arbitrary"), 32),
        name="pool_proj",
    )(d, w, scale, res)


def _attn_prompt_kernel(q_ref, k_ref, v_ref, o_ref):
    hd = MEM_HEAD_DIM
    for h in range(MEM_HEADS):
        q = q_ref[:, h * hd:(h + 1) * hd].astype(BF16)
        k = k_ref[:, h * hd:(h + 1) * hd].astype(BF16)
        v = v_ref[:, h * hd:(h + 1) * hd].astype(BF16)
        s = lax.dot_general(q, k, (((1,), (1,)), ((), ())), preferred_element_type=F32) * hd ** -0.5
        e = jnp.exp(s - jnp.max(s, axis=-1, keepdims=True))
        p = (e / jnp.sum(e, axis=-1, keepdims=True)).astype(BF16)
        o_ref[:, h * hd:(h + 1) * hd] = jnp.dot(p, v, preferred_element_type=F32).astype(BF16)


def _attn_prompt(q, k, v):
    tq = ATT_TQ
    nt = SEQ // tq
    return pl.pallas_call(
        _attn_prompt_kernel,
        grid=(BATCH, nt),
        in_specs=[pl.BlockSpec((tq, D_MODEL), lambda b, i: (b * nt + i, 0)),
                  pl.BlockSpec((None, N_MEM, D_MODEL), lambda b, i: (b, 0, 0)),
                  pl.BlockSpec((None, N_MEM, D_MODEL), lambda b, i: (b, 0, 0))],
        out_specs=pl.BlockSpec((tq, D_MODEL), lambda b, i: (b * nt + i, 0)),
        out_shape=jax.ShapeDtypeStruct((N_PROMPT, D_MODEL), BF16),
        compiler_params=_params(("parallel", "arbitrary"), 32),
        name="attn_prompt",
    )(q, k, v)


def _attn_sample_kernel(q_ref, k_ref, v_ref, o_ref):
    for b in range(ATT_BB):
        prod = k_ref[b] * q_ref[b][None]
        s = jnp.sum(prod, axis=-1, keepdims=True) * MEM_HEAD_DIM ** -0.5
        e = jnp.exp(s - jnp.max(s, axis=0, keepdims=True))
        p = e / jnp.sum(e, axis=0, keepdims=True)
        o_ref[b] = jnp.sum(p * v_ref[b], axis=0)


def _attn_sample(q, k, v, layer):
    bb = ATT_BB
    cache = pl.BlockSpec((None, bb, N_MEM, MEM_HEADS, MEM_HEAD_DIM), lambda i: (layer, i, 0, 0, 0))
    qo = pl.BlockSpec((bb, MEM_HEADS, MEM_HEAD_DIM), lambda i: (i, 0, 0))
    return pl.pallas_call(
        _attn_sample_kernel,
        grid=(DEC_BATCH // bb,),
        in_specs=[qo, cache, cache],
        out_specs=qo,
        out_shape=jax.ShapeDtypeStruct((DEC_BATCH, MEM_HEADS, MEM_HEAD_DIM), F32),
        compiler_params=_params(("parallel",), 48),
        name="attn_sample",
    )(q, k, v)


def _top16(s, ids, tiebreak):
    work = s
    rank = jnp.full(s.shape, float(PEER_TOPK), F32)
    vals = []
    for a in range(PEER_TOPK):
        m = jnp.max(work, axis=0, keepdims=True)
        sel = work == m
        if tiebreak:
            first = jnp.min(jnp.where(sel, ids, jnp.inf), axis=0, keepdims=True)
            sel = ids == first
        rank = jnp.where(sel, float(a), rank)
        work = jnp.where(sel, -jnp.inf, work)
        vals.append(m)
    return rank, vals


def _count_selected(rank):
    return jnp.sum((rank < float(PEER_TOPK)).astype(F32), axis=0, keepdims=True)


_CAND_ROWS = tuple(PEER_TOPK // (a + 1) for a in range(V7X_SUBLANES))


def _route_compute(q_ref, keys_ref, p_ref, t0_ref, qq_ref, r1_ref, tiebreak):
    tt = ROUTE_TT
    sub = V7X_SUBLANES
    iota_k = lax.broadcasted_iota(jnp.int32, (PEER_KEYS, tt), 0).astype(F32)
    iota_8 = lax.broadcasted_iota(jnp.int32, (sub, tt), 0).astype(F32)
    iota_16 = lax.broadcasted_iota(jnp.int32, (PEER_TOPK, tt), 0).astype(F32)
    cand_ids = jnp.concatenate(
        [iota_16] + [a * float(PEER_TOPK) + iota_8 for a in range(1, sub)]
        + [(iota_8 + float(sub)) * float(PEER_TOPK)], axis=0)
    bad = jnp.zeros((1, tt), F32)
    for h in range(PEER_HEADS):
        scores, ranks, svals = [], [], []
        for part in range(2):
            c0 = (2 * h + part) * PEER_HALF
            qh = q_ref[:, c0:c0 + PEER_HALF].astype(BF16)
            s = lax.dot_general(keys_ref[h, part], qh, (((1,), (1,)), ((), ())),
                                preferred_element_type=F32)
            rank, vals = _top16(s, iota_k, tiebreak)
            bad = bad + jnp.abs(_count_selected(rank) - float(PEER_TOPK))
            scores.append(s)
            ranks.append(rank)
            svals.append(vals)
        sv0 = jnp.concatenate(svals[0], axis=0)
        sv1 = jnp.concatenate(svals[1], axis=0)
        groups = [svals[0][0] + sv1]
        for a in range(1, sub):
            g = svals[0][a] + sv1[0:sub, :]
            groups.append(jnp.where(iota_8 < float(_CAND_ROWS[a]), g, -jnp.inf))
        groups.append(sv0[sub:, :] + svals[1][0])
        cand = jnp.concatenate(groups, axis=0)
        crank, _ = _top16(cand, cand_ids, tiebreak)
        selm = crank < float(PEER_TOPK)
        bad = bad + jnp.abs(_count_selected(crank) - float(PEER_TOPK))
        z = jnp.sum(jnp.where(selm, jnp.exp(cand - cand[0:1, :]), 0.0), axis=0, keepdims=True)
        e0 = jnp.where(ranks[0] < float(PEER_TOPK), jnp.exp(scores[0] - svals[0][0]), 0.0)
        e1 = jnp.where(ranks[1] < float(PEER_TOPK), jnp.exp(scores[1] - svals[1][0]), 0.0)
        kept = selm.astype(F32)
        counts = [jnp.sum(kept[0:PEER_TOPK, :], axis=0, keepdims=True)]
        for a in range(1, sub):
            r0 = PEER_TOPK + (a - 1) * sub
            counts.append(jnp.sum(kept[r0:r0 + sub, :], axis=0, keepdims=True))
        r0 = PEER_TOPK + (sub - 1) * sub
        counts += [kept[r0 + k:r0 + k + 1, :] for k in range(sub)]
        t0 = jnp.zeros((PEER_KEYS, tt), F32)
        for a in range(PEER_TOPK):
            t0 = jnp.where(ranks[0] == float(a), counts[a], t0)
        p_ref[h] = e0 / z
        t0_ref[h] = t0
        qq_ref[h] = e1.astype(BF16)
        r1_ref[h] = ranks[1].astype(BF16)
    return bad


def _route_kernel(q_ref, keys_ref, p_ref, t0_ref, qq_ref, r1_ref):
    outs = (p_ref, t0_ref, qq_ref, r1_ref)
    bad = _route_compute(q_ref, keys_ref, *outs, tiebreak=False)

    @pl.when(jnp.max(bad) > 0.0)
    def _():
        _route_compute(q_ref, keys_ref, *outs, tiebreak=True)


def _route(q, keys):
    tt = ROUTE_TT
    blk = pl.BlockSpec((PEER_HEADS, PEER_KEYS, tt), lambda i: (0, 0, i))
    shp = lambda dt: jax.ShapeDtypeStruct((PEER_HEADS, PEER_KEYS, N_TOK), dt)
    return pl.pallas_call(
        _route_kernel,
        grid=(N_TOK // tt,),
        in_specs=[pl.BlockSpec((tt, D_MODEL), lambda i: (i, 0)),
                  pl.BlockSpec((PEER_HEADS, 2, PEER_KEYS, PEER_HALF), lambda i: (0, 0, 0, 0))],
        out_specs=[blk, blk, blk, blk],
        out_shape=[shp(F32), shp(F32), shp(BF16), shp(BF16)],
        compiler_params=_params(("parallel",), 32),
        name="peer_route",
    )(q, keys)


_SCORE_COLS = ((0, 256), (256, 256), (512, TOK_TILE - 512))
_OUT_COLS = 512


def _peer_score_piece(u_ref, r0, z_ref, s_ref, c0, w):
    s_ref[:, c0:c0 + w] = lax.dot_general(
        u_ref[r0:r0 + PEER_EC, :], z_ref[c0:c0 + w, :], (((1,), (1,)), ((), ())),
        preferred_element_type=F32)
    return ((s_ref, 0, c0), (s_ref, PEER_EC - 1, c0 + w - V7X_LANES))


def _peer_out_piece(a_ref, v_ref, r0, o_ref, n):
    c0 = n * _OUT_COLS
    o_ref[:, c0:c0 + _OUT_COLS] += jnp.dot(a_ref[...], v_ref[r0:r0 + PEER_EC, c0:c0 + _OUT_COLS],
                                           preferred_element_type=F32)
    return ((o_ref, 0, c0), (o_ref, TOK_TILE - 1, c0 + _OUT_COLS - V7X_LANES))


def _zero_after(deps):
    zrow = None
    for ref, r, c in deps:
        bits = pltpu.bitcast(ref[r:r + 1, c:c + V7X_LANES], jnp.uint32)
        z = pltpu.bitcast(lax.shift_right_logical(bits, jnp.uint32(32)), F32)
        zrow = z if zrow is None else zrow + z
    return zrow


def _peer_act_piece(s_ref, p_ref, t0_ref, row, qq_ref, r1_ref, a_ref, il, deps):
    zero = jnp.zeros((), BF16)
    zrow = _zero_after(deps) if deps else None
    e0 = il * PEER_KEYS
    for n0, w in _SCORE_COLS:
        tok = slice(n0, n0 + w)
        g = None
        for h in range(PEER_HEADS):
            prow = p_ref[h, row:row + 1, tok]
            if zrow is not None:
                prow = prow + jnp.tile(zrow, (1, w // V7X_LANES))
            trow = t0_ref[h, row:row + 1, tok].astype(BF16)
            gh = jnp.where(r1_ref[h, :, tok] < trow, qq_ref[h, :, tok] * prow.astype(BF16), zero)
            g = gh if g is None else g + gh
        act = jax.nn.gelu(s_ref[e0:e0 + PEER_KEYS, tok], approximate=True).astype(BF16) * g
        a_ref[tok, e0:e0 + PEER_KEYS] = act.T


def _peer_phase(u_ref, ur0, z_ref, s_out, s_in, p_ref, t0_ref, row0, qq_ref, r1_ref, a_out,
                a_in, v_ref, vr0, o_ref, do_score, do_act, do_out):
    score = [functools.partial(_peer_score_piece, u_ref, ur0, z_ref, s_out, c0, w) for c0, w in _SCORE_COLS]
    out = [functools.partial(_peer_out_piece, a_in, v_ref, vr0, o_ref, n) for n in range(D_MODEL // _OUT_COLS)]
    nothing = lambda: ()
    score = score if do_score else [nothing] * len(score)
    out = out if do_out else [nothing] * len(out)
    mxu = [score[0], out[0], out[1], score[1], out[2], out[3], score[2]]
    done = []
    for k, piece in enumerate(mxu):
        done.append(piece())
        if do_act and k % 2 == 0 and k // 2 < PEER_CI:
            il = k // 2
            deps = done[k - 1] if k > 0 else ()
            _peer_act_piece(s_in, p_ref, t0_ref, row0 + il, qq_ref, r1_ref, a_out, il, deps)


def _peer_dense_kernel(z_ref, u_ref, v_ref, plo_ref, tlo_ref, phi_ref, thi_ref, qq_ref, r1_ref, res_ref,
                       o_ref, s0, s1, a0, a1):
    step = pl.program_id(1)
    last = pl.num_programs(1) - 1

    def body(first, final):
        if first:
            o_ref[...] = res_ref[...]
        _peer_phase(u_ref, 0, z_ref, s0, s1, plo_ref, tlo_ref, PEER_CI, qq_ref, r1_ref, a1,
                    a0, v_ref, 0, o_ref, do_score=not final, do_act=not first, do_out=not first)
        _peer_phase(u_ref, PEER_EC, z_ref, s1, s0, phi_ref, thi_ref, 0, qq_ref, r1_ref, a0,
                    a1, v_ref, PEER_EC, o_ref, do_score=not final, do_act=not final, do_out=not first)

    pl.when(step == 0)(functools.partial(body, True, False))
    pl.when(jnp.logical_and(step > 0, step < last))(functools.partial(body, False, False))
    pl.when(step == last)(functools.partial(body, False, True))


def _peer_dense(z, u, v, p, t0, qq, r1, res, layer):
    tt, ec = TOK_TILE, PEER_EC
    n_pairs = u.shape[1] // (2 * ec)
    once = pl.Buffered(1)
    lo = pl.BlockSpec((PEER_HEADS, 2 * PEER_CI, tt), lambda t, s: (0, jnp.maximum(s - 1, 0), t))
    hi = pl.BlockSpec((PEER_HEADS, 2 * PEER_CI, tt), lambda t, s: (0, jnp.minimum(s, n_pairs - 1), t))
    second = pl.BlockSpec((PEER_HEADS, PEER_KEYS, tt), lambda t, s: (0, 0, t))
    return pl.pallas_call(
        _peer_dense_kernel,
        grid=(N_TOK // tt, n_pairs + 1),
        in_specs=[pl.BlockSpec((tt, D_MODEL), lambda t, s: (t, 0)),
                  pl.BlockSpec((None, 2 * ec, D_MODEL), lambda t, s: (layer, jnp.minimum(s, n_pairs - 1), 0)),
                  pl.BlockSpec((None, 2 * ec, D_MODEL), lambda t, s: (layer, jnp.maximum(s - 1, 0), 0)),
                  lo, lo, hi, hi, second, second,
                  pl.BlockSpec((tt, D_MODEL), lambda t, s: (t, 0), pipeline_mode=once)],
        out_specs=pl.BlockSpec((tt, D_MODEL), lambda t, s: (t, 0)),
        out_shape=jax.ShapeDtypeStruct((N_TOK, D_MODEL), F32),
        scratch_shapes=[pltpu.VMEM((ec, tt), F32), pltpu.VMEM((ec, tt), F32),
                        pltpu.VMEM((tt, ec), BF16), pltpu.VMEM((tt, ec), BF16)],
        compiler_params=_params(("parallel", "arbitrary"), 56),
        name="peer_dense",
    )(z, u, v, p, t0, p, t0, qq, r1, res)


def _row(v):
    return v.reshape(1, -1)


def kernel(x_prompt, x_sample, mem_prompt, cache_mem_k, cache_mem_v, state_conv_a, state_conv_b, state_pool, norm_mix_g, norm_mem_g, norm_ffn_g, mem_norm_g, final_norm_g, w_in, conv_a_w, conv_a_b, ln_a_g, ln_a_b, conv_b_w, w_out, pool_w, pool_scale, mem_wq, mem_wk, mem_wv, mem_wo, peer_wq, peer_keys, peer_u, peer_v):
    bf = lambda a: a.astype(BF16)
    x = jnp.concatenate([x_prompt.reshape(N_PROMPT, D_MODEL), x_sample.reshape(DEC_BATCH, D_MODEL)], axis=0)

    mem = mem_prompt.reshape(BATCH * N_MEM, D_MODEL)
    pk, pv = [], []
    for l in range(DEPTH):
        wkv = jnp.concatenate([bf(mem_wk[l]), bf(mem_wv[l])], axis=1)
        kv = _linear(mem, wkv, g=_row(mem_norm_g[l]), tm=BATCH * N_MEM // 2)
        pk.append(kv[:, :D_MODEL].reshape(BATCH, N_MEM, D_MODEL))
        pv.append(kv[:, D_MODEL:].reshape(BATCH, N_MEM, D_MODEL))

    u_tab, v_tab = bf(peer_u), bf(peer_v)
    new_a, new_b, new_pool = [], [], []
    for l in range(DEPTH):
        if l % 2 == 0:
            e = l // 2
            p = _linear(x, bf(w_in[e]), g=_row(norm_mix_g[l]))
            cw = (conv_a_w[e], _row(conv_a_b[e]), _row(ln_a_g[e]), _row(ln_a_b[e]), conv_b_w[e])
            ab_p, a_tail, b_tail = _conv_prompt(p, *cw)
            ab_s, a_in_s, b_in_s = _conv_sample(p, state_conv_a[e], state_conv_b[e], *cw)
            x = _linear(jnp.concatenate([ab_p, ab_s], axis=0), bf(w_out[e]), res=x)
            new_a.append((a_tail[:, CONV_HALO - (CONV_A_WIDTH - 1):],
                          jnp.concatenate([state_conv_a[e][:, 1:], a_in_s[:, None]], axis=1)))
            new_b.append((b_tail[:, V7X_SUBLANES - (CONV_B_WIDTH - 1):],
                          jnp.concatenate([state_conv_b[e][:, 1:], b_in_s[:, None]], axis=1)))
        else:
            o = l // 2
            g = _row(norm_mix_g[l])
            d_p, h_tail = _pool_prompt(x, g)
            d_s, h_s = _pool_sample(x, state_pool[o], g)
            x = _pool_proj(jnp.concatenate([d_p, d_s], axis=0), bf(pool_w[o]), _row(pool_scale[o]), x)
            new_pool.append((h_tail[:, POOL_HALO - POOL_CTX:],
                             jnp.concatenate([state_pool[o][:, 1:], h_s[:, None]], axis=1)))

        q = _linear(x, bf(mem_wq[l]), g=_row(norm_mem_g[l]))
        o_p = _attn_prompt(q, pk[l], pv[l])
        o_s = _attn_sample(q[N_PROMPT:].reshape(DEC_BATCH, MEM_HEADS, MEM_HEAD_DIM), cache_mem_k, cache_mem_v, l)
        x = _linear(jnp.concatenate([o_p, bf(o_s.reshape(DEC_BATCH, D_MODEL))], axis=0), bf(mem_wo[l]), res=x)

        pq, z = _linear(x, bf(peer_wq[l]), g=_row(norm_ffn_g[l]), emit_h=True)
        gp, gt0, gq, gr1 = _route(pq, bf(peer_keys[l]))
        x = _peer_dense(z, u_tab, v_tab, gp, gt0, gq, gr1, x, l)

    y_prompt = _rmsnorm(x, _row(final_norm_g), N_PROMPT, 0, 512).reshape(BATCH, SEQ, D_MODEL)
    y_sample = _rmsnorm(x, _row(final_norm_g), DEC_BATCH, N_PROMPT, DEC_BATCH).reshape(DEC_BATCH, 1, D_MODEL)
    shape_kv = (DEPTH, BATCH, N_MEM, MEM_HEADS, MEM_HEAD_DIM)
    return (y_prompt, y_sample,
            jnp.stack(pk).reshape(shape_kv), jnp.stack(pv).reshape(shape_kv),
            jnp.stack([a for a, _ in new_a]), jnp.stack([b for b, _ in new_b]),
            jnp.stack([c for c, _ in new_pool]),
            jnp.stack([a for _, a in new_a]), jnp.stack([b for _, b in new_b]),
            jnp.stack([c for _, c in new_pool]))
```

```python
import functools

import jax
import jax.numpy as jnp
from jax import lax
from jax.experimental import pallas as pl
from jax.experimental.pallas import tpu as pltpu

F32 = jnp.float32
BF16 = jnp.bfloat16

D_MODEL = 2048
BATCH = 4
SEQ = 2048
DEPTH = 2
DEC_BATCH = 128
N_PROMPT = BATCH * SEQ
N_TOK = N_PROMPT + DEC_BATCH
D_A = D_MODEL // 2
D_B = D_MODEL // 2
CONV_A_WIDTH = 31
CONV_B_WIDTH = 3
IN_COLS = 2 * D_A + 3 * D_B
POOL_WINDOWS = (2, 4, 8, 16)
POOL_GROUP = D_MODEL // len(POOL_WINDOWS)
POOL_CTX = max(POOL_WINDOWS) - 1
N_MEM = 256
MEM_HEADS = 4
MEM_HEAD_DIM = D_MODEL // MEM_HEADS
PEER_HEADS = 8
PEER_KEYS = 128
PEER_TOPK = 16
PEER_HALF = 128
NORM_EPS = 1e-6
LN_EPS = 1e-5

V7X_LANES = 128
V7X_SUBLANES = 8
V7X_VMEM_BYTES = 64 * 1024 * 1024

TOK_TILE = 640
LIN_TN = 1024
CONV_TQ = 256
CONV_HALO = 32
CONV_ROWS = 32
POOL_TQ = 256
POOL_HALO = 16
ATT_TQ = 512
ATT_BB = 4
SAMPLE_BB = 32
ROUTE_TT = 128
PEER_CI = 4
PEER_EC = PEER_CI * PEER_KEYS

assert N_TOK % TOK_TILE == 0 and N_TOK % ROUTE_TT == 0


def _params(sem, vmem_mb):
    return pltpu.CompilerParams(dimension_semantics=sem, vmem_limit_bytes=vmem_mb * 1024 * 1024)


def _rms(x, g):
    return x * lax.rsqrt(jnp.mean(x * x, axis=-1, keepdims=True) + NORM_EPS) * g


def _linear_kernel(*refs, norm, has_res, emit_h):
    it = iter(refs)
    x_ref = next(it)
    g_ref = next(it) if norm else None
    w_ref = next(it)
    res_ref = next(it) if has_res else None
    o_ref = next(it)
    h_out_ref = next(it) if emit_h else None
    h_scr = next(it) if norm else None

    if norm:
        @pl.when(pl.program_id(1) == 0)
        def _():
            h = _rms(x_ref[...], g_ref[...]).astype(BF16)
            h_scr[...] = h
            if emit_h:
                h_out_ref[...] = h
        h = h_scr[...]
    else:
        h = x_ref[...].astype(BF16)
    y = jnp.dot(h, w_ref[...], preferred_element_type=F32)
    if has_res:
        y = res_ref[...] + y
    o_ref[...] = y.astype(o_ref.dtype)


def _linear(x, w, g=None, res=None, emit_h=False, rows=None, row_off=0, tm=TOK_TILE, tn=LIN_TN,
            out_dtype=F32):
    k, dout = w.shape
    rows = x.shape[0] if rows is None else rows
    assert rows % tm == 0 and dout % tn == 0 and row_off % tm == 0
    ro = row_off // tm
    norm = g is not None
    in_specs = [pl.BlockSpec((tm, k), lambda i, j: (i + ro, 0))]
    args = [x]
    if norm:
        in_specs.append(pl.BlockSpec((1, k), lambda i, j: (0, 0)))
        args.append(g)
    in_specs.append(pl.BlockSpec((k, tn), lambda i, j: (0, j)))
    args.append(w)
    if res is not None:
        in_specs.append(pl.BlockSpec((tm, tn), lambda i, j: (i + ro, j)))
        args.append(res)
    out_shape = [jax.ShapeDtypeStruct((rows, dout), out_dtype)]
    out_specs = [pl.BlockSpec((tm, tn), lambda i, j: (i, j))]
    if emit_h:
        out_shape.append(jax.ShapeDtypeStruct((rows, k), BF16))
        out_specs.append(pl.BlockSpec((tm, k), lambda i, j: (i, 0)))
    outs = pl.pallas_call(
        functools.partial(_linear_kernel, norm=norm, has_res=res is not None, emit_h=emit_h),
        grid=(rows // tm, dout // tn),
        in_specs=in_specs,
        out_specs=out_specs,
        out_shape=out_shape,
        scratch_shapes=[pltpu.VMEM((tm, k), BF16)] if norm else [],
        compiler_params=_params(("parallel", "arbitrary"), 48),
        name="linear",
    )(*args)
    return outs if emit_h else outs[0]


def _rmsnorm_kernel(x_ref, g_ref, o_ref):
    o_ref[...] = _rms(x_ref[...], g_ref[...])


def _rmsnorm(x, g, rows, row_off, tm):
    ro = row_off // tm
    return pl.pallas_call(
        _rmsnorm_kernel,
        grid=(rows // tm,),
        in_specs=[pl.BlockSpec((tm, D_MODEL), lambda i: (i + ro, 0)),
                  pl.BlockSpec((1, D_MODEL), lambda i: (0, 0))],
        out_specs=pl.BlockSpec((tm, D_MODEL), lambda i: (i, 0)),
        out_shape=jax.ShapeDtypeStruct((rows, D_MODEL), F32),
        compiler_params=_params(("parallel",), 32),
        name="final_norm",
    )(x, g)


def _layernorm_silu(a, g, b):
    mu = jnp.mean(a, axis=-1, keepdims=True)
    ac = a - mu
    var = jnp.mean(ac * ac, axis=-1, keepdims=True)
    y = ac * lax.rsqrt(var + LN_EPS) * g + b
    return y * jax.nn.sigmoid(y)


def _conv_prompt_kernel(av_ref, ag_ref, bg_ref, cg_ref, bx_ref,
                        hav_ref, hag_ref, hcg_ref, hbx_ref,
                        wa_ref, ba_ref, lg_ref, lb_ref, wb_ref,
                        ab_ref, atail_ref, btail_ref,
                        aext, bext, aconv, aphase):
    i = pl.program_id(1)
    tq, halo = CONV_TQ, CONV_HALO
    keep = (i > 0).astype(F32)
    a_in = av_ref[...] * jax.nn.sigmoid(ag_ref[...])
    b_in = cg_ref[...] * bx_ref[...]
    aext[0:halo, :] = hav_ref[...] * jax.nn.sigmoid(hag_ref[...]) * keep
    aext[halo:halo + tq, :] = a_in
    bext[0:halo, :] = hcg_ref[...] * hbx_ref[...] * keep
    bext[halo:halo + tq, :] = b_in

    sub = V7X_SUBLANES
    for rho in range(1, sub):
        aphase[rho - 1, 0:halo + tq - rho, :] = aext[rho:halo + tq, :]
    off_a = halo - (CONV_A_WIDTH - 1)
    for r0 in range(0, tq, CONV_ROWS):
        acc = jnp.broadcast_to(ba_ref[...], (CONV_ROWS, D_A))
        for k in range(CONV_A_WIDTH):
            rho = (off_a + k) % sub
            base = r0 + off_a + k - rho
            rows = aext[base:base + CONV_ROWS, :] if rho == 0 else aphase[rho - 1, base:base + CONV_ROWS, :]
            acc = acc + wa_ref[k:k + 1, :] * rows
        aconv[r0:r0 + CONV_ROWS, :] = acc
    a = _layernorm_silu(aconv[...], lg_ref[...], lb_ref[...])

    off_b = halo - (CONV_B_WIDTH - 1)
    cb = wb_ref[0:1, :] * bext[off_b:off_b + tq, :]
    for k in range(1, CONV_B_WIDTH):
        cb = cb + wb_ref[k:k + 1, :] * bext[off_b + k:off_b + k + tq, :]
    b = bg_ref[...] * cb

    ab_ref[:, 0:D_A] = a.astype(BF16)
    ab_ref[:, D_A:D_A + D_B] = b.astype(BF16)

    @pl.when(i == pl.num_programs(1) - 1)
    def _():
        atail_ref[...] = a_in[tq - CONV_HALO:, :]
        btail_ref[...] = b_in[tq - V7X_SUBLANES:, :]


def _conv_prompt(p, wa, ba, lg, lb, wb):
    tq, halo = CONV_TQ, CONV_HALO
    nt = SEQ // tq
    cur = lambda c: pl.BlockSpec((tq, D_A), lambda b, i, c=c: (b * nt + i, c))
    hal = lambda c: pl.BlockSpec(
        (halo, D_A), lambda b, i, c=c: (jnp.maximum((b * SEQ + i * tq) // halo - 1, 0), c))
    full = lambda r: pl.BlockSpec((r, D_A), lambda b, i: (0, 0))
    return pl.pallas_call(
        _conv_prompt_kernel,
        grid=(BATCH, nt),
        in_specs=[cur(0), cur(1), cur(2), cur(3), cur(4), hal(0), hal(1), hal(3), hal(4),
                  full(CONV_A_WIDTH), full(1), full(1), full(1), full(CONV_B_WIDTH)],
        out_specs=[pl.BlockSpec((tq, D_MODEL), lambda b, i: (b * nt + i, 0)),
                   pl.BlockSpec((None, CONV_HALO, D_A), lambda b, i: (b, 0, 0)),
                   pl.BlockSpec((None, V7X_SUBLANES, D_B), lambda b, i: (b, 0, 0))],
        out_shape=[jax.ShapeDtypeStruct((N_PROMPT, D_MODEL), BF16),
                   jax.ShapeDtypeStruct((BATCH, CONV_HALO, D_A), F32),
                   jax.ShapeDtypeStruct((BATCH, V7X_SUBLANES, D_B), F32)],
        scratch_shapes=[pltpu.VMEM((halo + tq, D_A), F32), pltpu.VMEM((halo + tq, D_B), F32),
                        pltpu.VMEM((tq, D_A), F32),
                        pltpu.VMEM((V7X_SUBLANES - 1, halo + tq, D_A), F32)],
        compiler_params=_params(("parallel", "arbitrary"), 48),
        name="conv_prompt",
    )(p, p, p, p, p, p, p, p, p, wa, ba, lg, lb, wb)


def _conv_sample_kernel(av_ref, ag_ref, bg_ref, cg_ref, bx_ref, sa_ref, sb_ref,
                        wa_ref, ba_ref, lg_ref, lb_ref, wb_ref,
                        ab_ref, ain_ref, bin_ref):
    na, nb = CONV_A_WIDTH - 1, CONV_B_WIDTH - 1
    a_in = av_ref[...] * jax.nn.sigmoid(ag_ref[...])
    b_in = cg_ref[...] * bx_ref[...]
    conv_a = jnp.sum(sa_ref[...] * wa_ref[0:na, :][None], axis=1) + wa_ref[na:na + 1, :] * a_in
    a = _layernorm_silu(conv_a + ba_ref[...], lg_ref[...], lb_ref[...])
    conv_b = jnp.sum(sb_ref[...] * wb_ref[0:nb, :][None], axis=1) + wb_ref[nb:nb + 1, :] * b_in
    b = bg_ref[...] * conv_b
    ab_ref[:, 0:D_A] = a.astype(BF16)
    ab_ref[:, D_A:D_A + D_B] = b.astype(BF16)
    ain_ref[...] = a_in
    bin_ref[...] = b_in


def _conv_sample(p, state_a, state_b, wa, ba, lg, lb, wb):
    bb = SAMPLE_BB
    r0 = N_PROMPT // bb
    cur = lambda c: pl.BlockSpec((bb, D_A), lambda i, c=c: (r0 + i, c))
    full = lambda r: pl.BlockSpec((r, D_A), lambda i: (0, 0))
    return pl.pallas_call(
        _conv_sample_kernel,
        grid=(DEC_BATCH // bb,),
        in_specs=[cur(0), cur(1), cur(2), cur(3), cur(4),
                  pl.BlockSpec((bb, CONV_A_WIDTH - 1, D_A), lambda i: (i, 0, 0)),
                  pl.BlockSpec((bb, CONV_B_WIDTH - 1, D_B), lambda i: (i, 0, 0)),
                  full(CONV_A_WIDTH), full(1), full(1), full(1), full(CONV_B_WIDTH)],
        out_specs=[pl.BlockSpec((bb, D_MODEL), lambda i: (i, 0)),
                   pl.BlockSpec((bb, D_A), lambda i: (i, 0)),
                   pl.BlockSpec((bb, D_B), lambda i: (i, 0))],
        out_shape=[jax.ShapeDtypeStruct((DEC_BATCH, D_MODEL), BF16),
                   jax.ShapeDtypeStruct((DEC_BATCH, D_A), F32),
                   jax.ShapeDtypeStruct((DEC_BATCH, D_B), F32)],
        compiler_params=_params(("parallel",), 32),
        name="conv_sample",
    )(p, p, p, p, p, state_a, state_b, wa, ba, lg, lb, wb)


def _pool_prompt_kernel(x_ref, hx_ref, g_ref, d_ref, tail_ref, ext):
    i = pl.program_id(1)
    tq, halo = POOL_TQ, POOL_HALO
    keep = (i > 0).astype(F32)
    h = _rms(x_ref[...], g_ref[...])
    ext[0:halo, :] = _rms(hx_ref[...], g_ref[...]) * keep
    ext[halo:halo + tq, :] = h
    pos = (i * tq + lax.broadcasted_iota(jnp.int32, (tq, 1), 0)).astype(F32)
    for gi, w in enumerate(POOL_WINDOWS):
        c0 = gi * POOL_GROUP
        s = h[:, c0:c0 + POOL_GROUP]
        for k in range(1, w):
            s = s + ext[halo - k:halo - k + tq, c0:c0 + POOL_GROUP]
        cnt = jnp.minimum(float(w), pos + 1.0)
        d_ref[:, c0:c0 + POOL_GROUP] = (s / cnt - h[:, c0:c0 + POOL_GROUP]).astype(BF16)

    @pl.when(i == pl.num_programs(1) - 1)
    def _():
        tail_ref[...] = h[tq - POOL_HALO:, :]


def _pool_prompt(x, g):
    tq, halo = POOL_TQ, POOL_HALO
    nt = SEQ // tq
    return pl.pallas_call(
        _pool_prompt_kernel,
        grid=(BATCH, nt),
        in_specs=[pl.BlockSpec((tq, D_MODEL), lambda b, i: (b * nt + i, 0)),
                  pl.BlockSpec((halo, D_MODEL),
                               lambda b, i: (jnp.maximum((b * SEQ + i * tq) // halo - 1, 0), 0)),
                  pl.BlockSpec((1, D_MODEL), lambda b, i: (0, 0))],
        out_specs=[pl.BlockSpec((tq, D_MODEL), lambda b, i: (b * nt + i, 0)),
                   pl.BlockSpec((None, POOL_HALO, D_MODEL), lambda b, i: (b, 0, 0))],
        out_shape=[jax.ShapeDtypeStruct((N_PROMPT, D_MODEL), BF16),
                   jax.ShapeDtypeStruct((BATCH, POOL_HALO, D_MODEL), F32)],
        scratch_shapes=[pltpu.VMEM((halo + tq, D_MODEL), F32)],
        compiler_params=_params(("parallel", "arbitrary"), 48),
        name="pool_prompt",
    )(x, x, g)


def _pool_sample_kernel(x_ref, st_ref, g_ref, d_ref, h_ref):
    h = _rms(x_ref[...], g_ref[...])
    h_ref[...] = h
    for gi, w in enumerate(POOL_WINDOWS):
        c0 = gi * POOL_GROUP
        s = h[:, c0:c0 + POOL_GROUP] + jnp.sum(
            st_ref[:, POOL_CTX - (w - 1):POOL_CTX, c0:c0 + POOL_GROUP], axis=1)
        d_ref[:, c0:c0 + POOL_GROUP] = (s / float(w) - h[:, c0:c0 + POOL_GROUP]).astype(BF16)


def _pool_sample(x, state, g):
    bb = SAMPLE_BB
    r0 = N_PROMPT // bb
    return pl.pallas_call(
        _pool_sample_kernel,
        grid=(DEC_BATCH // bb,),
        in_specs=[pl.BlockSpec((bb, D_MODEL), lambda i: (r0 + i, 0)),
                  pl.BlockSpec((bb, POOL_CTX, D_MODEL), lambda i: (i, 0, 0)),
                  pl.BlockSpec((1, D_MODEL), lambda i: (0, 0))],
        out_specs=[pl.BlockSpec((bb, D_MODEL), lambda i: (i, 0)),
                   pl.BlockSpec((bb, D_MODEL), lambda i: (i, 0))],
        out_shape=[jax.ShapeDtypeStruct((DEC_BATCH, D_MODEL), BF16),
                   jax.ShapeDtypeStruct((DEC_BATCH, D_MODEL), F32)],
        compiler_params=_params(("parallel",), 32),
        name="pool_sample",
    )(x, state, g)


def _pool_proj_kernel(d_ref, w_ref, sc_ref, res_ref, o_ref):
    y = jnp.dot(d_ref[...], w_ref[...], preferred_element_type=F32)
    o_ref[...] = res_ref[...] + y * sc_ref[...]


def _pool_proj(d, w, scale, res, tm=TOK_TILE):
    pg = POOL_GROUP
    return pl.pallas_call(
        _pool_proj_kernel,
        grid=(N_TOK // tm, len(POOL_WINDOWS)),
        in_specs=[pl.BlockSpec((tm, pg), lambda i, j: (i, j)),
                  pl.BlockSpec((None, pg, pg), lambda i, j: (j, 0, 0)),
                  pl.BlockSpec((1, pg), lambda i, j: (0, j)),
                  pl.BlockSpec((tm, pg), lambda i, j: (i, j))],
        out_specs=pl.BlockSpec((tm, pg), lambda i, j: (i, j)),
        out_shape=jax.ShapeDtypeStruct((N_TOK, D_MODEL), F32),
        compiler_params=_params(("parallel", "arbitrary"), 32),
        name="pool_proj",
    )(d, w, scale, res)


def _attn_prompt_kernel(q_ref, k_ref, v_ref, o_ref):
    hd = MEM_HEAD_DIM
    for h in range(MEM_HEADS):
        q = q_ref[:, h * hd:(h + 1) * hd].astype(BF16)
        k = k_ref[:, h * hd:(h + 1) * hd].astype(BF16)
        v = v_ref[:, h * hd:(h + 1) * hd].astype(BF16)
        s = lax.dot_general(q, k, (((1,), (1,)), ((), ())), preferred_element_type=F32) * hd ** -0.5
        e = jnp.exp(s - jnp.max(s, axis=-1, keepdims=True))
        p = (e / jnp.sum(e, axis=-1, keepdims=True)).astype(BF16)
        o_ref[:, h * hd:(h + 1) * hd] = jnp.dot(p, v, preferred_element_type=F32).astype(BF16)


def _attn_prompt(q, k, v):
    tq = ATT_TQ
    nt = SEQ // tq
    return pl.pallas_call(
        _attn_prompt_kernel,
        grid=(BATCH, nt),
        in_specs=[pl.BlockSpec((tq, D_MODEL), lambda b, i: (b * nt + i, 0)),
                  pl.BlockSpec((None, N_MEM, D_MODEL), lambda b, i: (b, 0, 0)),
                  pl.BlockSpec((None, N_MEM, D_MODEL), lambda b, i: (b, 0, 0))],
        out_specs=pl.BlockSpec((tq, D_MODEL), lambda b, i: (b * nt + i, 0)),
        out_shape=jax.ShapeDtypeStruct((N_PROMPT, D_MODEL), BF16),
        compiler_params=_params(("parallel", "arbitrary"), 32),
        name="attn_prompt",
    )(q, k, v)


def _attn_sample_kernel(q_ref, k_ref, v_ref, o_ref):
    for b in range(ATT_BB):
        prod = k_ref[b] * q_ref[b][None]
        s = jnp.sum(prod, axis=-1, keepdims=True) * MEM_HEAD_DIM ** -0.5
        e = jnp.exp(s - jnp.max(s, axis=0, keepdims=True))
        p = e / jnp.sum(e, axis=0, keepdims=True)
        o_ref[b] = jnp.sum(p * v_ref[b], axis=0)


def _attn_sample(q, k, v, layer):
    bb = ATT_BB
    cache = pl.BlockSpec((None, bb, N_MEM, MEM_HEADS, MEM_HEAD_DIM), lambda i: (layer, i, 0, 0, 0))
    qo = pl.BlockSpec((bb, MEM_HEADS, MEM_HEAD_DIM), lambda i: (i, 0, 0))
    return pl.pallas_call(
        _attn_sample_kernel,
        grid=(DEC_BATCH // bb,),
        in_specs=[qo, cache, cache],
        out_specs=qo,
        out_shape=jax.ShapeDtypeStruct((DEC_BATCH, MEM_HEADS, MEM_HEAD_DIM), F32),
        compiler_params=_params(("parallel",), 48),
        name="attn_sample",
    )(q, k, v)


def _top16(s, ids, tiebreak):
    work = s
    rank = jnp.full(s.shape, float(PEER_TOPK), F32)
    vals = []
    for a in range(PEER_TOPK):
        m = jnp.max(work, axis=0, keepdims=True)
        sel = work == m
        if tiebreak:
            first = jnp.min(jnp.where(sel, ids, jnp.inf), axis=0, keepdims=True)
            sel = ids == first
        rank = jnp.where(sel, float(a), rank)
        work = jnp.where(sel, -jnp.inf, work)
        vals.append(m)
    return rank, vals


def _count_selected(rank):
    return jnp.sum((rank < float(PEER_TOPK)).astype(F32), axis=0, keepdims=True)


_CAND_ROWS = tuple(PEER_TOPK // (a + 1) for a in range(V7X_SUBLANES))


def _route_compute(q_ref, keys_ref, p_ref, t0_ref, qq_ref, r1_ref, tiebreak):
    tt = ROUTE_TT
    sub = V7X_SUBLANES
    iota_k = lax.broadcasted_iota(jnp.int32, (PEER_KEYS, tt), 0).astype(F32)
    iota_8 = lax.broadcasted_iota(jnp.int32, (sub, tt), 0).astype(F32)
    iota_16 = lax.broadcasted_iota(jnp.int32, (PEER_TOPK, tt), 0).astype(F32)
    cand_ids = jnp.concatenate(
        [iota_16] + [a * float(PEER_TOPK) + iota_8 for a in range(1, sub)]
        + [(iota_8 + float(sub)) * float(PEER_TOPK)], axis=0)
    bad = jnp.zeros((1, tt), F32)
    for h in range(PEER_HEADS):
        scores, ranks, svals = [], [], []
        for part in range(2):
            c0 = (2 * h + part) * PEER_HALF
            qh = q_ref[:, c0:c0 + PEER_HALF].astype(BF16)
            s = lax.dot_general(keys_ref[h, part], qh, (((1,), (1,)), ((), ())),
                                preferred_element_type=F32)
            rank, vals = _top16(s, iota_k, tiebreak)
            bad = bad + jnp.abs(_count_selected(rank) - float(PEER_TOPK))
            scores.append(s)
            ranks.append(rank)
            svals.append(vals)
        sv0 = jnp.concatenate(svals[0], axis=0)
        sv1 = jnp.concatenate(svals[1], axis=0)
        groups = [svals[0][0] + sv1]
        for a in range(1, sub):
            g = svals[0][a] + sv1[0:sub, :]
            groups.append(jnp.where(iota_8 < float(_CAND_ROWS[a]), g, -jnp.inf))
        groups.append(sv0[sub:, :] + svals[1][0])
        cand = jnp.concatenate(groups, axis=0)
        crank, _ = _top16(cand, cand_ids, tiebreak)
        selm = crank < float(PEER_TOPK)
        bad = bad + jnp.abs(_count_selected(crank) - float(PEER_TOPK))
        z = jnp.sum(jnp.where(selm, jnp.exp(cand - cand[0:1, :]), 0.0), axis=0, keepdims=True)
        e0 = jnp.where(ranks[0] < float(PEER_TOPK), jnp.exp(scores[0] - svals[0][0]), 0.0)
        e1 = jnp.where(ranks[1] < float(PEER_TOPK), jnp.exp(scores[1] - svals[1][0]), 0.0)
        kept = selm.astype(F32)
        counts = [jnp.sum(kept[0:PEER_TOPK, :], axis=0, keepdims=True)]
        for a in range(1, sub):
            r0 = PEER_TOPK + (a - 1) * sub
            counts.append(jnp.sum(kept[r0:r0 + sub, :], axis=0, keepdims=True))
        r0 = PEER_TOPK + (sub - 1) * sub
        counts += [kept[r0 + k:r0 + k + 1, :] for k in range(sub)]
        t0 = jnp.zeros((PEER_KEYS, tt), F32)
        for a in range(PEER_TOPK):
            t0 = jnp.where(ranks[0] == float(a), counts[a], t0)
        p_ref[h] = e0 / z
        t0_ref[h] = t0
        qq_ref[h] = e1.astype(BF16)
        r1_ref[h] = ranks[1].astype(BF16)
    return bad


def _route_kernel(q_ref, keys_ref, p_ref, t0_ref, qq_ref, r1_ref):
    outs = (p_ref, t0_ref, qq_ref, r1_ref)
    bad = _route_compute(q_ref, keys_ref, *outs, tiebreak=False)

    @pl.when(jnp.max(bad) > 0.0)
    def _():
        _route_compute(q_ref, keys_ref, *outs, tiebreak=True)


def _route(q, keys):
    tt = ROUTE_TT
    blk = pl.BlockSpec((PEER_HEADS, PEER_KEYS, tt), lambda i: (0, 0, i))
    shp = lambda dt: jax.ShapeDtypeStruct((PEER_HEADS, PEER_KEYS, N_TOK), dt)
    return pl.pallas_call(
        _route_kernel,
        grid=(N_TOK // tt,),
        in_specs=[pl.BlockSpec((tt, D_MODEL), lambda i: (i, 0)),
                  pl.BlockSpec((PEER_HEADS, 2, PEER_KEYS, PEER_HALF), lambda i: (0, 0, 0, 0))],
        out_specs=[blk, blk, blk, blk],
        out_shape=[shp(F32), shp(F32), shp(BF16), shp(BF16)],
        compiler_params=_params(("parallel",), 32),
        name="peer_route",
    )(q, keys)


_SCORE_COLS = ((0, 256), (256, 256), (512, TOK_TILE - 512))
_OUT_COLS = 512


def _peer_score_piece(u_ref, r0, z_ref, s_ref, c0, w):
    s_ref[:, c0:c0 + w] = lax.dot_general(
        u_ref[r0:r0 + PEER_EC, :], z_ref[c0:c0 + w, :], (((1,), (1,)), ((), ())),
        preferred_element_type=F32)
    return ((s_ref, 0, c0), (s_ref, PEER_EC - 1, c0 + w - V7X_LANES))


def _peer_out_piece(a_ref, v_ref, r0, o_ref, n):
    c0 = n * _OUT_COLS
    o_ref[:, c0:c0 + _OUT_COLS] += jnp.dot(a_ref[...], v_ref[r0:r0 + PEER_EC, c0:c0 + _OUT_COLS],
                                           preferred_element_type=F32)
    return ((o_ref, 0, c0), (o_ref, TOK_TILE - 1, c0 + _OUT_COLS - V7X_LANES))


def _zero_after(deps):
    zrow = None
    for ref, r, c in deps:
        bits = pltpu.bitcast(ref[r:r + 1, c:c + V7X_LANES], jnp.uint32)
        z = pltpu.bitcast(lax.shift_right_logical(bits, jnp.uint32(32)), F32)
        zrow = z if zrow is None else zrow + z
    return zrow


def _peer_act_piece(s_ref, p_ref, t0_ref, row, qq_ref, r1_ref, a_ref, il, deps):
    zero = jnp.zeros((), BF16)
    zrow = _zero_after(deps) if deps else None
    e0 = il * PEER_KEYS
    for n0, w in _SCORE_COLS:
        tok = slice(n0, n0 + w)
        g = None
        for h in range(PEER_HEADS):
            prow = p_ref[h, row:row + 1, tok]
            if zrow is not None:
                prow = prow + jnp.tile(zrow, (1, w // V7X_LANES))
            trow = t0_ref[h, row:row + 1, tok].astype(BF16)
            gh = jnp.where(r1_ref[h, :, tok] < trow, qq_ref[h, :, tok] * prow.astype(BF16), zero)
            g = gh if g is None else g + gh
        act = jax.nn.gelu(s_ref[e0:e0 + PEER_KEYS, tok], approximate=True).astype(BF16) * g
        a_ref[tok, e0:e0 + PEER_KEYS] = act.T


def _peer_phase(u_ref, ur0, z_ref, s_out, s_in, p_ref, t0_ref, row0, qq_ref, r1_ref, a_out,
                a_in, v_ref, vr0, o_ref, do_score, do_act, do_out):
    score = [functools.partial(_peer_score_piece, u_ref, ur0, z_ref, s_out, c0, w) for c0, w in _SCORE_COLS]
    out = [functools.partial(_peer_out_piece, a_in, v_ref, vr0, o_ref, n) for n in range(D_MODEL // _OUT_COLS)]
    nothing = lambda: ()
    score = score if do_score else [nothing] * len(score)
    out = out if do_out else [nothing] * len(out)
    mxu = [score[0], out[0], out[1], score[1], out[2], out[3], score[2]]
    done = []
    for k, piece in enumerate(mxu):
        done.append(piece())
        if do_act and k % 2 == 0 and k // 2 < PEER_CI:
            il = k // 2
            deps = done[k - 1] if k > 0 else ()
            _peer_act_piece(s_in, p_ref, t0_ref, row0 + il, qq_ref, r1_ref, a_out, il, deps)


def _peer_dense_kernel(z_ref, u_ref, v_ref, plo_ref, tlo_ref, phi_ref, thi_ref, qq_ref, r1_ref, res_ref,
                       o_ref, s0, s1, a0, a1):
    step = pl.program_id(1)
    last = pl.num_programs(1) - 1

    def body(first, final):
        if first:
            o_ref[...] = res_ref[...]
        _peer_phase(u_ref, 0, z_ref, s0, s1, plo_ref, tlo_ref, PEER_CI, qq_ref, r1_ref, a1,
                    a0, v_ref, 0, o_ref, do_score=not final, do_act=not first, do_out=not first)
        _peer_phase(u_ref, PEER_EC, z_ref, s1, s0, phi_ref, thi_ref, 0, qq_ref, r1_ref, a0,
                    a1, v_ref, PEER_EC, o_ref, do_score=not final, do_act=not final, do_out=not first)

    pl.when(step == 0)(functools.partial(body, True, False))
    pl.when(jnp.logical_and(step > 0, step < last))(functools.partial(body, False, False))
    pl.when(step == last)(functools.partial(body, False, True))


def _peer_dense(z, u, v, p, t0, qq, r1, res, layer):
    tt, ec = TOK_TILE, PEER_EC
    n_pairs = u.shape[1] // (2 * ec)
    lo = pl.BlockSpec((PEER_HEADS, 2 * PEER_CI, tt), lambda t, s: (0, jnp.maximum(s - 1, 0), t))
    hi = pl.BlockSpec((PEER_HEADS, 2 * PEER_CI, tt), lambda t, s: (0, jnp.minimum(s, n_pairs - 1), t))
    second = pl.BlockSpec((PEER_HEADS, PEER_KEYS, tt), lambda t, s: (0, 0, t))
    return pl.pallas_call(
        _peer_dense_kernel,
        grid=(N_TOK // tt, n_pairs + 1),
        in_specs=[pl.BlockSpec((tt, D_MODEL), lambda t, s: (t, 0)),
                  pl.BlockSpec((None, 2 * ec, D_MODEL), lambda t, s: (layer, jnp.minimum(s, n_pairs - 1), 0)),
                  pl.BlockSpec((None, 2 * ec, D_MODEL), lambda t, s: (layer, jnp.maximum(s - 1, 0), 0)),
                  lo, lo, hi, hi, second, second,
                  pl.BlockSpec((tt, D_MODEL), lambda t, s: (t, 0))],
        out_specs=pl.BlockSpec((tt, D_MODEL), lambda t, s: (t, 0)),
        out_shape=jax.ShapeDtypeStruct((N_TOK, D_MODEL), F32),
        scratch_shapes=[pltpu.VMEM((ec, tt), F32), pltpu.VMEM((ec, tt), F32),
                        pltpu.VMEM((tt, ec), BF16), pltpu.VMEM((tt, ec), BF16)],
        compiler_params=_params(("parallel", "arbitrary"), 56),
        name="peer_dense",
    )(z, u, v, p, t0, p, t0, qq, r1, res)


def _row(v):
    return v.reshape(1, -1)


def kernel(x_prompt, x_sample, mem_prompt, cache_mem_k, cache_mem_v, state_conv_a, state_conv_b, state_pool, norm_mix_g, norm_mem_g, norm_ffn_g, mem_norm_g, final_norm_g, w_in, conv_a_w, conv_a_b, ln_a_g, ln_a_b, conv_b_w, w_out, pool_w, pool_scale, mem_wq, mem_wk, mem_wv, mem_wo, peer_wq, peer_keys, peer_u, peer_v):
    bf = lambda a: a.astype(BF16)
    x = jnp.concatenate([x_prompt.reshape(N_PROMPT, D_MODEL), x_sample.reshape(DEC_BATCH, D_MODEL)], axis=0)

    mem = mem_prompt.reshape(BATCH * N_MEM, D_MODEL)
    pk, pv = [], []
    for l in range(DEPTH):
        wkv = jnp.concatenate([bf(mem_wk[l]), bf(mem_wv[l])], axis=1)
        kv = _linear(mem, wkv, g=_row(mem_norm_g[l]), tm=BATCH * N_MEM // 2)
        pk.append(kv[:, :D_MODEL].reshape(BATCH, N_MEM, D_MODEL))
        pv.append(kv[:, D_MODEL:].reshape(BATCH, N_MEM, D_MODEL))

    u_tab, v_tab = bf(peer_u), bf(peer_v)
    new_a, new_b, new_pool = [], [], []
    for l in range(DEPTH):
        if l % 2 == 0:
            e = l // 2
            p = _linear(x, bf(w_in[e]), g=_row(norm_mix_g[l]))
            cw = (conv_a_w[e], _row(conv_a_b[e]), _row(ln_a_g[e]), _row(ln_a_b[e]), conv_b_w[e])
            ab_p, a_tail, b_tail = _conv_prompt(p, *cw)
            ab_s, a_in_s, b_in_s = _conv_sample(p, state_conv_a[e], state_conv_b[e], *cw)
            x = _linear(jnp.concatenate([ab_p, ab_s], axis=0), bf(w_out[e]), res=x)
            new_a.append((a_tail[:, CONV_HALO - (CONV_A_WIDTH - 1):],
                          jnp.concatenate([state_conv_a[e][:, 1:], a_in_s[:, None]], axis=1)))
            new_b.append((b_tail[:, V7X_SUBLANES - (CONV_B_WIDTH - 1):],
                          jnp.concatenate([state_conv_b[e][:, 1:], b_in_s[:, None]], axis=1)))
        else:
            o = l // 2
            g = _row(norm_mix_g[l])
            d_p, h_tail = _pool_prompt(x, g)
            d_s, h_s = _pool_sample(x, state_pool[o], g)
            x = _pool_proj(jnp.concatenate([d_p, d_s], axis=0), bf(pool_w[o]), _row(pool_scale[o]), x)
            new_pool.append((h_tail[:, POOL_HALO - POOL_CTX:],
                             jnp.concatenate([state_pool[o][:, 1:], h_s[:, None]], axis=1)))

        q = _linear(x, bf(mem_wq[l]), g=_row(norm_mem_g[l]), out_dtype=BF16)
        o_p = _attn_prompt(q, pk[l], pv[l])
        q_s = q[N_PROMPT:].astype(F32).reshape(DEC_BATCH, MEM_HEADS, MEM_HEAD_DIM)
        o_s = _attn_sample(q_s, cache_mem_k, cache_mem_v, l)
        x = _linear(jnp.concatenate([o_p, bf(o_s.reshape(DEC_BATCH, D_MODEL))], axis=0), bf(mem_wo[l]), res=x)

        pq, z = _linear(x, bf(peer_wq[l]), g=_row(norm_ffn_g[l]), emit_h=True, out_dtype=BF16)
        gp, gt0, gq, gr1 = _route(pq, bf(peer_keys[l]))
        x = _peer_dense(z, u_tab, v_tab, gp, gt0, gq, gr1, x, l)

    y_prompt = _rmsnorm(x, _row(final_norm_g), N_PROMPT, 0, 512).reshape(BATCH, SEQ, D_MODEL)
    y_sample = _rmsnorm(x, _row(final_norm_g), DEC_BATCH, N_PROMPT, DEC_BATCH).reshape(DEC_BATCH, 1, D_MODEL)
    shape_kv = (DEPTH, BATCH, N_MEM, MEM_HEADS, MEM_HEAD_DIM)
    return (y_prompt, y_sample,
            jnp.stack(pk).reshape(shape_kv), jnp.stack(pv).reshape(shape_kv),
            jnp.stack([a for a, _ in new_a]), jnp.stack([b for b, _ in new_b]),
            jnp.stack([c for c, _ in new_pool]),
            jnp.stack([a for _, a in new_a]), jnp.stack([b for _, b in new_b]),
            jnp.stack([c for _, c in new_pool]))
```

```python
import functools

import jax
import jax.numpy as jnp
from jax import lax
from jax.experimental import pallas as pl
from jax.experimental.pallas import tpu as pltpu

F32 = jnp.float32
BF16 = jnp.bfloat16

D_MODEL = 2048
BATCH = 4
SEQ = 2048
DEPTH = 2
DEC_BATCH = 128
N_PROMPT = BATCH * SEQ
N_TOK = N_PROMPT + DEC_BATCH
D_A = D_MODEL // 2
D_B = D_MODEL // 2
CONV_A_WIDTH = 31
CONV_B_WIDTH = 3
IN_COLS = 2 * D_A + 3 * D_B
POOL_WINDOWS = (2, 4, 8, 16)
POOL_GROUP = D_MODEL // len(POOL_WINDOWS)
POOL_CTX = max(POOL_WINDOWS) - 1
N_MEM = 256
MEM_HEADS = 4
MEM_HEAD_DIM = D_MODEL // MEM_HEADS
PEER_HEADS = 8
PEER_KEYS = 128
PEER_TOPK = 16
PEER_HALF = 128
NORM_EPS = 1e-6
LN_EPS = 1e-5

V7X_LANES = 128
V7X_SUBLANES = 8
V7X_VMEM_BYTES = 64 * 1024 * 1024

TOK_TILE = 640
LIN_TN = 2048
CONV_TQ = 256
CONV_HALO = 32
CONV_ROWS = 32
POOL_TQ = 256
POOL_HALO = 16
ATT_TQ = 512
ATT_BB = 4
SAMPLE_BB = 32
ROUTE_TT = 128
PEER_CI = 4
PEER_EC = PEER_CI * PEER_KEYS

assert N_TOK % TOK_TILE == 0 and N_TOK % ROUTE_TT == 0


def _params(sem, vmem_mb):
    return pltpu.CompilerParams(dimension_semantics=sem, vmem_limit_bytes=vmem_mb * 1024 * 1024)


def _rms(x, g):
    return x * lax.rsqrt(jnp.mean(x * x, axis=-1, keepdims=True) + NORM_EPS) * g


def _linear_kernel(*refs, norm, has_res, emit_h):
    it = iter(refs)
    x_ref = next(it)
    g_ref = next(it) if norm else None
    w_ref = next(it)
    res_ref = next(it) if has_res else None
    o_ref = next(it)
    h_out_ref = next(it) if emit_h else None
    h_scr = next(it) if norm else None

    if norm:
        @pl.when(pl.program_id(1) == 0)
        def _():
            h = _rms(x_ref[...], g_ref[...]).astype(BF16)
            h_scr[...] = h
            if emit_h:
                h_out_ref[...] = h
        h = h_scr[...]
    else:
        h = x_ref[...].astype(BF16)
    y = jnp.dot(h, w_ref[...], preferred_element_type=F32)
    if has_res:
        y = res_ref[...] + y
    o_ref[...] = y.astype(o_ref.dtype)


def _linear(x, w, g=None, res=None, emit_h=False, rows=None, row_off=0, tm=TOK_TILE, tn=LIN_TN,
            out_dtype=F32):
    k, dout = w.shape
    rows = x.shape[0] if rows is None else rows
    assert rows % tm == 0 and dout % tn == 0 and row_off % tm == 0
    ro = row_off // tm
    norm = g is not None
    in_specs = [pl.BlockSpec((tm, k), lambda i, j: (i + ro, 0))]
    args = [x]
    if norm:
        in_specs.append(pl.BlockSpec((1, k), lambda i, j: (0, 0)))
        args.append(g)
    in_specs.append(pl.BlockSpec((k, tn), lambda i, j: (0, j)))
    args.append(w)
    if res is not None:
        in_specs.append(pl.BlockSpec((tm, tn), lambda i, j: (i + ro, j)))
        args.append(res)
    out_shape = [jax.ShapeDtypeStruct((rows, dout), out_dtype)]
    out_specs = [pl.BlockSpec((tm, tn), lambda i, j: (i, j))]
    if emit_h:
        out_shape.append(jax.ShapeDtypeStruct((rows, k), BF16))
        out_specs.append(pl.BlockSpec((tm, k), lambda i, j: (i, 0)))
    outs = pl.pallas_call(
        functools.partial(_linear_kernel, norm=norm, has_res=res is not None, emit_h=emit_h),
        grid=(rows // tm, dout // tn),
        in_specs=in_specs,
        out_specs=out_specs,
        out_shape=out_shape,
        scratch_shapes=[pltpu.VMEM((tm, k), BF16)] if norm else [],
        compiler_params=_params(("parallel", "arbitrary"), 48),
        name="linear",
    )(*args)
    return outs if emit_h else outs[0]


def _rmsnorm_kernel(x_ref, g_ref, o_ref):
    o_ref[...] = _rms(x_ref[...], g_ref[...])


def _rmsnorm(x, g, rows, row_off, tm):
    ro = row_off // tm
    return pl.pallas_call(
        _rmsnorm_kernel,
        grid=(rows // tm,),
        in_specs=[pl.BlockSpec((tm, D_MODEL), lambda i: (i + ro, 0)),
                  pl.BlockSpec((1, D_MODEL), lambda i: (0, 0))],
        out_specs=pl.BlockSpec((tm, D_MODEL), lambda i: (i, 0)),
        out_shape=jax.ShapeDtypeStruct((rows, D_MODEL), F32),
        compiler_params=_params(("parallel",), 32),
        name="final_norm",
    )(x, g)


def _layernorm_silu(a, g, b):
    mu = jnp.mean(a, axis=-1, keepdims=True)
    ac = a - mu
    var = jnp.mean(ac * ac, axis=-1, keepdims=True)
    y = ac * lax.rsqrt(var + LN_EPS) * g + b
    return y * jax.nn.sigmoid(y)


def _conv_prompt_kernel(av_ref, ag_ref, bg_ref, cg_ref, bx_ref,
                        hav_ref, hag_ref, hcg_ref, hbx_ref,
                        wa_ref, ba_ref, lg_ref, lb_ref, wb_ref,
                        ab_ref, atail_ref, btail_ref,
                        aext, bext, aconv, aphase):
    i = pl.program_id(1)
    tq, halo = CONV_TQ, CONV_HALO
    keep = (i > 0).astype(F32)
    a_in = av_ref[...] * jax.nn.sigmoid(ag_ref[...])
    b_in = cg_ref[...] * bx_ref[...]
    aext[0:halo, :] = hav_ref[...] * jax.nn.sigmoid(hag_ref[...]) * keep
    aext[halo:halo + tq, :] = a_in
    bext[0:halo, :] = hcg_ref[...] * hbx_ref[...] * keep
    bext[halo:halo + tq, :] = b_in

    sub = V7X_SUBLANES
    for rho in range(1, sub):
        aphase[rho - 1, 0:halo + tq - rho, :] = aext[rho:halo + tq, :]
    off_a = halo - (CONV_A_WIDTH - 1)
    for r0 in range(0, tq, CONV_ROWS):
        acc = jnp.broadcast_to(ba_ref[...], (CONV_ROWS, D_A))
        for k in range(CONV_A_WIDTH):
            rho = (off_a + k) % sub
            base = r0 + off_a + k - rho
            rows = aext[base:base + CONV_ROWS, :] if rho == 0 else aphase[rho - 1, base:base + CONV_ROWS, :]
            acc = acc + wa_ref[k:k + 1, :] * rows
        aconv[r0:r0 + CONV_ROWS, :] = acc
    a = _layernorm_silu(aconv[...], lg_ref[...], lb_ref[...])

    off_b = halo - (CONV_B_WIDTH - 1)
    cb = wb_ref[0:1, :] * bext[off_b:off_b + tq, :]
    for k in range(1, CONV_B_WIDTH):
        cb = cb + wb_ref[k:k + 1, :] * bext[off_b + k:off_b + k + tq, :]
    b = bg_ref[...] * cb

    ab_ref[:, 0:D_A] = a.astype(BF16)
    ab_ref[:, D_A:D_A + D_B] = b.astype(BF16)

    @pl.when(i == pl.num_programs(1) - 1)
    def _():
        atail_ref[...] = a_in[tq - CONV_HALO:, :]
        btail_ref[...] = b_in[tq - V7X_SUBLANES:, :]


def _conv_prompt(p, wa, ba, lg, lb, wb):
    tq, halo = CONV_TQ, CONV_HALO
    nt = SEQ // tq
    cur = lambda c: pl.BlockSpec((tq, D_A), lambda b, i, c=c: (b * nt + i, c))
    hal = lambda c: pl.BlockSpec(
        (halo, D_A), lambda b, i, c=c: (jnp.maximum((b * SEQ + i * tq) // halo - 1, 0), c))
    full = lambda r: pl.BlockSpec((r, D_A), lambda b, i: (0, 0))
    return pl.pallas_call(
        _conv_prompt_kernel,
        grid=(BATCH, nt),
        in_specs=[cur(0), cur(1), cur(2), cur(3), cur(4), hal(0), hal(1), hal(3), hal(4),
                  full(CONV_A_WIDTH), full(1), full(1), full(1), full(CONV_B_WIDTH)],
        out_specs=[pl.BlockSpec((tq, D_MODEL), lambda b, i: (b * nt + i, 0)),
                   pl.BlockSpec((None, CONV_HALO, D_A), lambda b, i: (b, 0, 0)),
                   pl.BlockSpec((None, V7X_SUBLANES, D_B), lambda b, i: (b, 0, 0))],
        out_shape=[jax.ShapeDtypeStruct((N_PROMPT, D_MODEL), BF16),
                   jax.ShapeDtypeStruct((BATCH, CONV_HALO, D_A), F32),
                   jax.ShapeDtypeStruct((BATCH, V7X_SUBLANES, D_B), F32)],
        scratch_shapes=[pltpu.VMEM((halo + tq, D_A), F32), pltpu.VMEM((halo + tq, D_B), F32),
                        pltpu.VMEM((tq, D_A), F32),
                        pltpu.VMEM((V7X_SUBLANES - 1, halo + tq, D_A), F32)],
        compiler_params=_params(("parallel", "arbitrary"), 48),
        name="conv_prompt",
    )(p, p, p, p, p, p, p, p, p, wa, ba, lg, lb, wb)


def _conv_sample_kernel(av_ref, ag_ref, bg_ref, cg_ref, bx_ref, sa_ref, sb_ref,
                        wa_ref, ba_ref, lg_ref, lb_ref, wb_ref,
                        ab_ref, ain_ref, bin_ref):
    na, nb = CONV_A_WIDTH - 1, CONV_B_WIDTH - 1
    a_in = av_ref[...] * jax.nn.sigmoid(ag_ref[...])
    b_in = cg_ref[...] * bx_ref[...]
    conv_a = jnp.sum(sa_ref[...] * wa_ref[0:na, :][None], axis=1) + wa_ref[na:na + 1, :] * a_in
    a = _layernorm_silu(conv_a + ba_ref[...], lg_ref[...], lb_ref[...])
    conv_b = jnp.sum(sb_ref[...] * wb_ref[0:nb, :][None], axis=1) + wb_ref[nb:nb + 1, :] * b_in
    b = bg_ref[...] * conv_b
    ab_ref[:, 0:D_A] = a.astype(BF16)
    ab_ref[:, D_A:D_A + D_B] = b.astype(BF16)
    ain_ref[...] = a_in
    bin_ref[...] = b_in


def _conv_sample(p, state_a, state_b, wa, ba, lg, lb, wb):
    bb = SAMPLE_BB
    r0 = N_PROMPT // bb
    cur = lambda c: pl.BlockSpec((bb, D_A), lambda i, c=c: (r0 + i, c))
    full = lambda r: pl.BlockSpec((r, D_A), lambda i: (0, 0))
    return pl.pallas_call(
        _conv_sample_kernel,
        grid=(DEC_BATCH // bb,),
        in_specs=[cur(0), cur(1), cur(2), cur(3), cur(4),
                  pl.BlockSpec((bb, CONV_A_WIDTH - 1, D_A), lambda i: (i, 0, 0)),
                  pl.BlockSpec((bb, CONV_B_WIDTH - 1, D_B), lambda i: (i, 0, 0)),
                  full(CONV_A_WIDTH), full(1), full(1), full(1), full(CONV_B_WIDTH)],
        out_specs=[pl.BlockSpec((bb, D_MODEL), lambda i: (i, 0)),
                   pl.BlockSpec((bb, D_A), lambda i: (i, 0)),
                   pl.BlockSpec((bb, D_B), lambda i: (i, 0))],
        out_shape=[jax.ShapeDtypeStruct((DEC_BATCH, D_MODEL), BF16),
                   jax.ShapeDtypeStruct((DEC_BATCH, D_A), F32),
                   jax.ShapeDtypeStruct((DEC_BATCH, D_B), F32)],
        compiler_params=_params(("parallel",), 32),
        name="conv_sample",
    )(p, p, p, p, p, state_a, state_b, wa, ba, lg, lb, wb)


def _pool_prompt_kernel(x_ref, hx_ref, g_ref, d_ref, tail_ref, ext):
    i = pl.program_id(1)
    tq, halo = POOL_TQ, POOL_HALO
    keep = (i > 0).astype(F32)
    h = _rms(x_ref[...], g_ref[...])
    ext[0:halo, :] = _rms(hx_ref[...], g_ref[...]) * keep
    ext[halo:halo + tq, :] = h
    pos = (i * tq + lax.broadcasted_iota(jnp.int32, (tq, 1), 0)).astype(F32)
    for gi, w in enumerate(POOL_WINDOWS):
        c0 = gi * POOL_GROUP
        s = h[:, c0:c0 + POOL_GROUP]
        for k in range(1, w):
            s = s + ext[halo - k:halo - k + tq, c0:c0 + POOL_GROUP]
        cnt = jnp.minimum(float(w), pos + 1.0)
        d_ref[:, c0:c0 + POOL_GROUP] = (s / cnt - h[:, c0:c0 + POOL_GROUP]).astype(BF16)

    @pl.when(i == pl.num_programs(1) - 1)
    def _():
        tail_ref[...] = h[tq - POOL_HALO:, :]


def _pool_prompt(x, g):
    tq, halo = POOL_TQ, POOL_HALO
    nt = SEQ // tq
    return pl.pallas_call(
        _pool_prompt_kernel,
        grid=(BATCH, nt),
        in_specs=[pl.BlockSpec((tq, D_MODEL), lambda b, i: (b * nt + i, 0)),
                  pl.BlockSpec((halo, D_MODEL),
                               lambda b, i: (jnp.maximum((b * SEQ + i * tq) // halo - 1, 0), 0)),
                  pl.BlockSpec((1, D_MODEL), lambda b, i: (0, 0))],
        out_specs=[pl.BlockSpec((tq, D_MODEL), lambda b, i: (b * nt + i, 0)),
                   pl.BlockSpec((None, POOL_HALO, D_MODEL), lambda b, i: (b, 0, 0))],
        out_shape=[jax.ShapeDtypeStruct((N_PROMPT, D_MODEL), BF16),
                   jax.ShapeDtypeStruct((BATCH, POOL_HALO, D_MODEL), F32)],
        scratch_shapes=[pltpu.VMEM((halo + tq, D_MODEL), F32)],
        compiler_params=_params(("parallel", "arbitrary"), 48),
        name="pool_prompt",
    )(x, x, g)


def _pool_sample_kernel(x_ref, st_ref, g_ref, d_ref, h_ref):
    h = _rms(x_ref[...], g_ref[...])
    h_ref[...] = h
    for gi, w in enumerate(POOL_WINDOWS):
        c0 = gi * POOL_GROUP
        s = h[:, c0:c0 + POOL_GROUP] + jnp.sum(
            st_ref[:, POOL_CTX - (w - 1):POOL_CTX, c0:c0 + POOL_GROUP], axis=1)
        d_ref[:, c0:c0 + POOL_GROUP] = (s / float(w) - h[:, c0:c0 + POOL_GROUP]).astype(BF16)


def _pool_sample(x, state, g):
    bb = SAMPLE_BB
    r0 = N_PROMPT // bb
    return pl.pallas_call(
        _pool_sample_kernel,
        grid=(DEC_BATCH // bb,),
        in_specs=[pl.BlockSpec((bb, D_MODEL), lambda i: (r0 + i, 0)),
                  pl.BlockSpec((bb, POOL_CTX, D_MODEL), lambda i: (i, 0, 0)),
                  pl.BlockSpec((1, D_MODEL), lambda i: (0, 0))],
        out_specs=[pl.BlockSpec((bb, D_MODEL), lambda i: (i, 0)),
                   pl.BlockSpec((bb, D_MODEL), lambda i: (i, 0))],
        out_shape=[jax.ShapeDtypeStruct((DEC_BATCH, D_MODEL), BF16),
                   jax.ShapeDtypeStruct((DEC_BATCH, D_MODEL), F32)],
        compiler_params=_params(("parallel",), 32),
        name="pool_sample",
    )(x, state, g)


def _pool_proj_kernel(d_ref, w_ref, sc_ref, res_ref, o_ref):
    y = jnp.dot(d_ref[...], w_ref[...], preferred_element_type=F32)
    o_ref[...] = res_ref[...] + y * sc_ref[...]


def _pool_proj(d, w, scale, res, tm=TOK_TILE):
    pg = POOL_GROUP
    return pl.pallas_call(
        _pool_proj_kernel,
        grid=(N_TOK // tm, len(POOL_WINDOWS)),
        in_specs=[pl.BlockSpec((tm, pg), lambda i, j: (i, j)),
                  pl.BlockSpec((None, pg, pg), lambda i, j: (j, 0, 0)),
                  pl.BlockSpec((1, pg), lambda i, j: (0, j)),
                  pl.BlockSpec((tm, pg), lambda i, j: (i, j))],
        out_specs=pl.BlockSpec((tm, pg), lambda i, j: (i, j)),
        out_shape=jax.ShapeDtypeStruct((N_TOK, D_MODEL), F32),
        compiler_params=_params(("parallel", "arbitrary"), 32),
        name="pool_proj",
    )(d, w, scale, res)


def _attn_prompt_kernel(q_ref, k_ref, v_ref, o_ref):
    hd = MEM_HEAD_DIM
    for h in range(MEM_HEADS):
        q = q_ref[:, h * hd:(h + 1) * hd].astype(BF16)
        k = k_ref[:, h * hd:(h + 1) * hd].astype(BF16)
        v = v_ref[:, h * hd:(h + 1) * hd].astype(BF16)
        s = lax.dot_general(q, k, (((1,), (1,)), ((), ())), preferred_element_type=F32) * hd ** -0.5
        e = jnp.exp(s - jnp.max(s, axis=-1, keepdims=True))
        p = (e / jnp.sum(e, axis=-1, keepdims=True)).astype(BF16)
        o_ref[:, h * hd:(h + 1) * hd] = jnp.dot(p, v, preferred_element_type=F32).astype(BF16)


def _attn_prompt(q, k, v):
    tq = ATT_TQ
    nt = SEQ // tq
    return pl.pallas_call(
        _attn_prompt_kernel,
        grid=(BATCH, nt),
        in_specs=[pl.BlockSpec((tq, D_MODEL), lambda b, i: (b * nt + i, 0)),
                  pl.BlockSpec((None, N_MEM, D_MODEL), lambda b, i: (b, 0, 0)),
                  pl.BlockSpec((None, N_MEM, D_MODEL), lambda b, i: (b, 0, 0))],
        out_specs=pl.BlockSpec((tq, D_MODEL), lambda b, i: (b * nt + i, 0)),
        out_shape=jax.ShapeDtypeStruct((N_PROMPT, D_MODEL), BF16),
        compiler_params=_params(("parallel", "arbitrary"), 32),
        name="attn_prompt",
    )(q, k, v)


def _attn_sample_kernel(q_ref, k_ref, v_ref, o_ref):
    for b in range(ATT_BB):
        prod = k_ref[b] * q_ref[b][None]
        s = jnp.sum(prod, axis=-1, keepdims=True) * MEM_HEAD_DIM ** -0.5
        e = jnp.exp(s - jnp.max(s, axis=0, keepdims=True))
        p = e / jnp.sum(e, axis=0, keepdims=True)
        o_ref[b] = jnp.sum(p * v_ref[b], axis=0)


def _attn_sample(q, k, v, layer):
    bb = ATT_BB
    cache = pl.BlockSpec((None, bb, N_MEM, MEM_HEADS, MEM_HEAD_DIM), lambda i: (layer, i, 0, 0, 0))
    qo = pl.BlockSpec((bb, MEM_HEADS, MEM_HEAD_DIM), lambda i: (i, 0, 0))
    return pl.pallas_call(
        _attn_sample_kernel,
        grid=(DEC_BATCH // bb,),
        in_specs=[qo, cache, cache],
        out_specs=qo,
        out_shape=jax.ShapeDtypeStruct((DEC_BATCH, MEM_HEADS, MEM_HEAD_DIM), F32),
        compiler_params=_params(("parallel",), 48),
        name="attn_sample",
    )(q, k, v)


def _top16(s, ids, tiebreak, want_rank):
    want_rank = want_rank or tiebreak
    work = s
    rank = jnp.full(s.shape, float(PEER_TOPK), F32) if want_rank else None
    vals = []
    for a in range(PEER_TOPK):
        m = jnp.max(work, axis=0, keepdims=True)
        sel = work == m
        if tiebreak:
            first = jnp.min(jnp.where(sel, ids, jnp.inf), axis=0, keepdims=True)
            sel = ids == first
        if want_rank:
            rank = jnp.where(sel, float(a), rank)
        work = jnp.where(sel, -jnp.inf, work)
        vals.append(m)
    selected = rank < float(PEER_TOPK) if want_rank else s >= vals[-1]
    return selected, rank, vals


def _excess_selected(selected):
    return jnp.abs(jnp.sum(selected.astype(F32), axis=0, keepdims=True) - float(PEER_TOPK))


_CAND_ROWS = tuple(PEER_TOPK // (a + 1) for a in range(V7X_SUBLANES))


def _route_compute(q_ref, keys_ref, p_ref, t0_ref, qq_ref, r1_ref, tiebreak):
    tt = ROUTE_TT
    sub = V7X_SUBLANES
    iota_k = lax.broadcasted_iota(jnp.int32, (PEER_KEYS, tt), 0).astype(F32)
    iota_8 = lax.broadcasted_iota(jnp.int32, (sub, tt), 0).astype(F32)
    iota_16 = lax.broadcasted_iota(jnp.int32, (PEER_TOPK, tt), 0).astype(F32)
    cand_ids = jnp.concatenate(
        [iota_16] + [a * float(PEER_TOPK) + iota_8 for a in range(1, sub)]
        + [(iota_8 + float(sub)) * float(PEER_TOPK)], axis=0)
    bad = jnp.zeros((1, tt), F32)
    for h in range(PEER_HEADS):
        scores, picked, ranks, svals = [], [], [], []
        for part in range(2):
            c0 = (2 * h + part) * PEER_HALF
            qh = q_ref[:, c0:c0 + PEER_HALF].astype(BF16)
            s = lax.dot_general(keys_ref[h, part], qh, (((1,), (1,)), ((), ())),
                                preferred_element_type=F32)
            sel, rank, vals = _top16(s, iota_k, tiebreak, want_rank=part == 1)
            bad = bad + _excess_selected(sel)
            scores.append(s)
            picked.append(sel)
            ranks.append(rank)
            svals.append(vals)
        sv0 = jnp.concatenate(svals[0], axis=0)
        sv1 = jnp.concatenate(svals[1], axis=0)
        groups = [svals[0][0] + sv1]
        for a in range(1, sub):
            g = svals[0][a] + sv1[0:sub, :]
            groups.append(jnp.where(iota_8 < float(_CAND_ROWS[a]), g, -jnp.inf))
        groups.append(sv0[sub:, :] + svals[1][0])
        cand = jnp.concatenate(groups, axis=0)
        selm, _, _ = _top16(cand, cand_ids, tiebreak, want_rank=False)
        bad = bad + _excess_selected(selm)
        z = jnp.sum(jnp.where(selm, jnp.exp(cand - cand[0:1, :]), 0.0), axis=0, keepdims=True)
        e0 = jnp.where(picked[0], jnp.exp(scores[0] - svals[0][0]), 0.0)
        e1 = jnp.where(picked[1], jnp.exp(scores[1] - svals[1][0]), 0.0)
        kept = selm.astype(F32)
        counts = [jnp.sum(kept[0:PEER_TOPK, :], axis=0, keepdims=True)]
        for a in range(1, sub):
            r0 = PEER_TOPK + (a - 1) * sub
            counts.append(jnp.sum(kept[r0:r0 + sub, :], axis=0, keepdims=True))
        r0 = PEER_TOPK + (sub - 1) * sub
        counts += [kept[r0 + k:r0 + k + 1, :] for k in range(sub)]
        t0 = jnp.zeros((PEER_KEYS, tt), F32)
        for a in range(PEER_TOPK):
            hit = scores[0] == svals[0][a] if ranks[0] is None else ranks[0] == float(a)
            t0 = jnp.where(hit, counts[a], t0)
        p_ref[h] = e0 / z
        t0_ref[h] = t0
        qq_ref[h] = e1.astype(BF16)
        r1_ref[h] = ranks[1].astype(BF16)
    return bad


def _route_kernel(q_ref, keys_ref, p_ref, t0_ref, qq_ref, r1_ref):
    outs = (p_ref, t0_ref, qq_ref, r1_ref)
    bad = _route_compute(q_ref, keys_ref, *outs, tiebreak=False)

    @pl.when(jnp.max(bad) > 0.0)
    def _():
        _route_compute(q_ref, keys_ref, *outs, tiebreak=True)


def _route(q, keys):
    tt = ROUTE_TT
    blk = pl.BlockSpec((PEER_HEADS, PEER_KEYS, tt), lambda i: (0, 0, i))
    shp = lambda dt: jax.ShapeDtypeStruct((PEER_HEADS, PEER_KEYS, N_TOK), dt)
    return pl.pallas_call(
        _route_kernel,
        grid=(N_TOK // tt,),
        in_specs=[pl.BlockSpec((tt, D_MODEL), lambda i: (i, 0)),
                  pl.BlockSpec((PEER_HEADS, 2, PEER_KEYS, PEER_HALF), lambda i: (0, 0, 0, 0))],
        out_specs=[blk, blk, blk, blk],
        out_shape=[shp(F32), shp(F32), shp(BF16), shp(BF16)],
        compiler_params=_params(("parallel",), 32),
        name="peer_route",
    )(q, keys)


_SCORE_COLS = ((0, 256), (256, 256), (512, TOK_TILE - 512))
_OUT_COLS = 512


def _peer_score_piece(u_ref, r0, z_ref, s_ref, c0, w):
    s_ref[:, c0:c0 + w] = lax.dot_general(
        u_ref[r0:r0 + PEER_EC, :], z_ref[c0:c0 + w, :], (((1,), (1,)), ((), ())),
        preferred_element_type=F32)
    return ((s_ref, 0, c0), (s_ref, PEER_EC - 1, c0 + w - V7X_LANES))


def _peer_out_piece(a_ref, v_ref, r0, o_ref, n):
    c0 = n * _OUT_COLS
    o_ref[:, c0:c0 + _OUT_COLS] += jnp.dot(a_ref[...], v_ref[r0:r0 + PEER_EC, c0:c0 + _OUT_COLS],
                                           preferred_element_type=F32)
    return ((o_ref, 0, c0), (o_ref, TOK_TILE - 1, c0 + _OUT_COLS - V7X_LANES))


def _zero_after(deps):
    zrow = None
    for ref, r, c in deps:
        bits = pltpu.bitcast(ref[r:r + 1, c:c + V7X_LANES], jnp.uint32)
        z = pltpu.bitcast(lax.shift_right_logical(bits, jnp.uint32(32)), F32)
        zrow = z if zrow is None else zrow + z
    return zrow


def _rows_bf16(row):
    packed = 2 * V7X_SUBLANES
    tile = jnp.broadcast_to(row, (packed, row.shape[1])).astype(BF16)
    return jnp.tile(tile, (PEER_KEYS // packed, 1))


def _peer_act_piece(s_ref, p_ref, t0_ref, row0, qq_ref, r1_ref, a_ref, n0, w, deps):
    zero = jnp.zeros((), BF16)
    zrow = _zero_after(deps) if deps else None
    tok = slice(n0, n0 + w)
    for il in range(PEER_CI):
        e0 = il * PEER_KEYS
        g = None
        for h in range(PEER_HEADS):
            prow = p_ref[h, row0 + il:row0 + il + 1, tok]
            if zrow is not None:
                prow = prow + jnp.tile(zrow, (1, w // V7X_LANES))
            trow = t0_ref[h, row0 + il:row0 + il + 1, tok]
            gh = jnp.where(r1_ref[h, :, tok] < _rows_bf16(trow), qq_ref[h, :, tok] * _rows_bf16(prow), zero)
            g = gh if g is None else g + gh
        act = jax.nn.gelu(s_ref[e0:e0 + PEER_KEYS, tok], approximate=True).astype(BF16) * g
        a_ref[tok, e0:e0 + PEER_KEYS] = act.T


def _peer_phase(u_ref, ur0, z_ref, s_out, s_in, p_ref, t0_ref, row0, qq_ref, r1_ref, a_out,
                a_in, v_ref, vr0, o_ref, do_score, do_act, do_out):
    score = [functools.partial(_peer_score_piece, u_ref, ur0, z_ref, s_out, c0, w) for c0, w in _SCORE_COLS]
    out = [functools.partial(_peer_out_piece, a_in, v_ref, vr0, o_ref, n) for n in range(D_MODEL // _OUT_COLS)]
    nothing = lambda: ()
    score = score if do_score else [nothing] * len(score)
    out = out if do_out else [nothing] * len(out)
    mxu = [score[0], out[0], out[1], score[1], out[2], out[3], score[2]]
    done = []
    for k, piece in enumerate(mxu):
        done.append(piece())
        if do_act and k % 3 == 0:
            n0, w = _SCORE_COLS[k // 3]
            deps = done[k - 1] if k > 0 else ()
            _peer_act_piece(s_in, p_ref, t0_ref, row0, qq_ref, r1_ref, a_out, n0, w, deps)


def _peer_dense_kernel(z_ref, u_ref, v_ref, plo_ref, tlo_ref, phi_ref, thi_ref, qq_ref, r1_ref, res_ref,
                       o_ref, s0, s1, a0, a1):
    step = pl.program_id(1)
    last = pl.num_programs(1) - 1

    def body(first, final):
        if first:
            o_ref[...] = res_ref[...]
        _peer_phase(u_ref, 0, z_ref, s0, s1, plo_ref, tlo_ref, PEER_CI, qq_ref, r1_ref, a1,
                    a0, v_ref, 0, o_ref, do_score=not final, do_act=not first, do_out=not first)
        _peer_phase(u_ref, PEER_EC, z_ref, s1, s0, phi_ref, thi_ref, 0, qq_ref, r1_ref, a0,
                    a1, v_ref, PEER_EC, o_ref, do_score=not final, do_act=not final, do_out=not first)

    pl.when(step == 0)(functools.partial(body, True, False))
    pl.when(jnp.logical_and(step > 0, step < last))(functools.partial(body, False, False))
    pl.when(step == last)(functools.partial(body, False, True))


def _peer_dense(z, u, v, p, t0, qq, r1, res, layer):
    tt, ec = TOK_TILE, PEER_EC
    n_pairs = u.shape[1] // (2 * ec)
    lo = pl.BlockSpec((PEER_HEADS, 2 * PEER_CI, tt), lambda t, s: (0, jnp.maximum(s - 1, 0), t))
    hi = pl.BlockSpec((PEER_HEADS, 2 * PEER_CI, tt), lambda t, s: (0, jnp.minimum(s, n_pairs - 1), t))
    second = pl.BlockSpec((PEER_HEADS, PEER_KEYS, tt), lambda t, s: (0, 0, t))
    return pl.pallas_call(
        _peer_dense_kernel,
        grid=(N_TOK // tt, n_pairs + 1),
        in_specs=[pl.BlockSpec((tt, D_MODEL), lambda t, s: (t, 0)),
                  pl.BlockSpec((None, 2 * ec, D_MODEL), lambda t, s: (layer, jnp.minimum(s, n_pairs - 1), 0)),
                  pl.BlockSpec((None, 2 * ec, D_MODEL), lambda t, s: (layer, jnp.maximum(s - 1, 0), 0)),
                  lo, lo, hi, hi, second, second,
                  pl.BlockSpec((tt, D_MODEL), lambda t, s: (t, 0))],
        out_specs=pl.BlockSpec((tt, D_MODEL), lambda t, s: (t, 0)),
        out_shape=jax.ShapeDtypeStruct((N_TOK, D_MODEL), F32),
        scratch_shapes=[pltpu.VMEM((ec, tt), F32), pltpu.VMEM((ec, tt), F32),
                        pltpu.VMEM((tt, ec), BF16), pltpu.VMEM((tt, ec), BF16)],
        compiler_params=_params(("parallel", "arbitrary"), 56),
        name="peer_dense",
    )(z, u, v, p, t0, p, t0, qq, r1, res)


def _row(v):
    return v.reshape(1, -1)


def kernel(x_prompt, x_sample, mem_prompt, cache_mem_k, cache_mem_v, state_conv_a, state_conv_b, state_pool, norm_mix_g, norm_mem_g, norm_ffn_g, mem_norm_g, final_norm_g, w_in, conv_a_w, conv_a_b, ln_a_g, ln_a_b, conv_b_w, w_out, pool_w, pool_scale, mem_wq, mem_wk, mem_wv, mem_wo, peer_wq, peer_keys, peer_u, peer_v):
    bf = lambda a: a.astype(BF16)
    x = jnp.concatenate([x_prompt.reshape(N_PROMPT, D_MODEL), x_sample.reshape(DEC_BATCH, D_MODEL)], axis=0)

    mem = mem_prompt.reshape(BATCH * N_MEM, D_MODEL)
    pk, pv = [], []
    for l in range(DEPTH):
        wkv = jnp.concatenate([bf(mem_wk[l]), bf(mem_wv[l])], axis=1)
        kv = _linear(mem, wkv, g=_row(mem_norm_g[l]), tm=BATCH * N_MEM // 2)
        pk.append(kv[:, :D_MODEL].reshape(BATCH, N_MEM, D_MODEL))
        pv.append(kv[:, D_MODEL:].reshape(BATCH, N_MEM, D_MODEL))

    u_tab, v_tab = bf(peer_u), bf(peer_v)
    new_a, new_b, new_pool = [], [], []
    for l in range(DEPTH):
        if l % 2 == 0:
            e = l // 2
            p = _linear(x, bf(w_in[e]), g=_row(norm_mix_g[l]), tn=IN_COLS // 4)
            cw = (conv_a_w[e], _row(conv_a_b[e]), _row(ln_a_g[e]), _row(ln_a_b[e]), conv_b_w[e])
            ab_p, a_tail, b_tail = _conv_prompt(p, *cw)
            ab_s, a_in_s, b_in_s = _conv_sample(p, state_conv_a[e], state_conv_b[e], *cw)
            x = _linear(jnp.concatenate([ab_p, ab_s], axis=0), bf(w_out[e]), res=x)
            new_a.append((a_tail[:, CONV_HALO - (CONV_A_WIDTH - 1):],
                          jnp.concatenate([state_conv_a[e][:, 1:], a_in_s[:, None]], axis=1)))
            new_b.append((b_tail[:, V7X_SUBLANES - (CONV_B_WIDTH - 1):],
                          jnp.concatenate([state_conv_b[e][:, 1:], b_in_s[:, None]], axis=1)))
        else:
            o = l // 2
            g = _row(norm_mix_g[l])
            d_p, h_tail = _pool_prompt(x, g)
            d_s, h_s = _pool_sample(x, state_pool[o], g)
            x = _pool_proj(jnp.concatenate([d_p, d_s], axis=0), bf(pool_w[o]), _row(pool_scale[o]), x)
            new_pool.append((h_tail[:, POOL_HALO - POOL_CTX:],
                             jnp.concatenate([state_pool[o][:, 1:], h_s[:, None]], axis=1)))

        q = _linear(x, bf(mem_wq[l]), g=_row(norm_mem_g[l]), out_dtype=BF16)
        o_p = _attn_prompt(q, pk[l], pv[l])
        q_s = q[N_PROMPT:].astype(F32).reshape(DEC_BATCH, MEM_HEADS, MEM_HEAD_DIM)
        o_s = _attn_sample(q_s, cache_mem_k, cache_mem_v, l)
        x = _linear(jnp.concatenate([o_p, bf(o_s.reshape(DEC_BATCH, D_MODEL))], axis=0), bf(mem_wo[l]), res=x)

        pq, z = _linear(x, bf(peer_wq[l]), g=_row(norm_ffn_g[l]), emit_h=True, out_dtype=BF16)
        gp, gt0, gq, gr1 = _route(pq, bf(peer_keys[l]))
        x = _peer_dense(z, u_tab, v_tab, gp, gt0, gq, gr1, x, l)

    y_prompt = _rmsnorm(x, _row(final_norm_g), N_PROMPT, 0, 512).reshape(BATCH, SEQ, D_MODEL)
    y_sample = _rmsnorm(x, _row(final_norm_g), DEC_BATCH, N_PROMPT, DEC_BATCH).reshape(DEC_BATCH, 1, D_MODEL)
    shape_kv = (DEPTH, BATCH, N_MEM, MEM_HEADS, MEM_HEAD_DIM)
    return (y_prompt, y_sample,
            jnp.stack(pk).reshape(shape_kv), jnp.stack(pv).reshape(shape_kv),
            jnp.stack([a for a, _ in new_a]), jnp.stack([b for b, _ in new_b]),
            jnp.stack([c for c, _ in new_pool]),
            jnp.stack([a for _, a in new_a]), jnp.stack([b for _, b in new_b]),
            jnp.stack([c for _, c in new_pool]))
```
